```python
import math
import jax, jax.numpy as jnp
from jax import lax
import numpy as np


D_MODEL = 1024
BATCH = 4
SEQ = 8192
DEPTH = 2

CHUNK = 64
CONV_WIDTH = 4
RMS_EPS = 1e-6
GDN_HEADS = D_MODEL // 256
GDN_DK = 128
GDN_DV = 128
GDN_QK = GDN_HEADS * GDN_DK
GDN_V = GDN_HEADS * GDN_DV
SSD_HEADDIM = 64
SSD_HEADS = D_MODEL // 128
SSD_INNER = SSD_HEADS * SSD_HEADDIM
SSD_GROUPS = 2
SSD_STATE = 128
SSD_BC = SSD_GROUPS * SSD_STATE
LRU_WIDTH = D_MODEL // 2
LRU_BLOCKS = 8
LRU_BLOCK = LRU_WIDTH // LRU_BLOCKS
LRU_C = 8.0
N_BRANCH = 3
BRANCH_WIDTH = 512
D_FF = 4 * D_MODEL
N_MOD = 6
IN_SPLITS = (GDN_QK, GDN_QK, GDN_V, GDN_V, GDN_HEADS, GDN_HEADS,
             SSD_INNER, SSD_INNER, SSD_BC, SSD_BC, SSD_HEADS,
             LRU_WIDTH, LRU_WIDTH,
             N_BRANCH * D_MODEL)
D_IN = sum(IN_SPLITS)

kernel_name = 'hybrid_gdn_ssd_rglru_adaln_block'


def rmsnorm(x, w):
    xf = x.astype(jnp.float32)
    y = xf * lax.rsqrt(jnp.mean(xf * xf, axis=-1, keepdims=True) + RMS_EPS)
    return (y * w.astype(jnp.float32)).astype(x.dtype)


def l2norm(x):
    return x * lax.rsqrt(jnp.sum(x * x, axis=-1, keepdims=True) + RMS_EPS)


def split_cols(t, sizes):
    idx = np.cumsum(sizes)[:-1].tolist()
    return jnp.split(t, idx, axis=-1)


def causal_conv(x, w):
    width = w.shape[0]
    seq = x.shape[1]
    xp = jnp.pad(x, ((0, 0), (width - 1, 0), (0, 0)))
    return sum(xp[:, k:k + seq] * w[k] for k in range(width))


def gated_deltanet(q, k, v, z, b_raw, a_raw, a_log, dt_bias, norm_w):
    f32 = jnp.float32
    bsz, seq, _ = q.shape
    nc = seq // CHUNK

    def heads(t, d):
        return t.astype(f32).reshape(bsz, nc, CHUNK, GDN_HEADS, d).transpose(0, 1, 3, 2, 4)

    def per_head(t):
        return t.astype(f32).reshape(bsz, nc, CHUNK, GDN_HEADS).transpose(0, 1, 3, 2)

    q = l2norm(heads(q, GDN_DK)) * (GDN_DK ** -0.5)
    k = l2norm(heads(k, GDN_DK))
    v = heads(v, GDN_DV)
    beta = jax.nn.sigmoid(per_head(b_raw))
    g = -jnp.exp(a_log.astype(f32))[:, None] * jax.nn.softplus(per_head(a_raw) + dt_bias.astype(f32)[:, None])
    gcum = jnp.cumsum(g, axis=-1)
    causal = jnp.tril(jnp.ones((CHUNK, CHUNK), bool))
    strict = jnp.tril(jnp.ones((CHUNK, CHUNK), bool), -1)
    decay = jnp.exp(jnp.where(causal, gcum[..., :, None] - gcum[..., None, :], -jnp.inf))
    kk = jnp.einsum('bnhcd,bnhed->bnhce', k, k)
    m = jnp.where(strict, beta[..., :, None] * kk * decay, 0.0)
    eye = jnp.eye(CHUNK, dtype=f32)
    rhs = jnp.concatenate([beta[..., None] * v, (beta * jnp.exp(gcum))[..., None] * k], axis=-1)
    sol = lax.linalg.triangular_solve(eye + m, rhs, left_side=True, lower=True, unit_diagonal=True)
    u, w = sol[..., :GDN_DV], sol[..., GDN_DV:]
    qk = jnp.einsum('bnhcd,bnhed->bnhce', q, k) * decay
    q_dec = q * jnp.exp(gcum)[..., None]
    k_dec = k * jnp.exp(gcum[..., -1:] - gcum)[..., None]
    g_tot = jnp.exp(gcum[..., -1])

    def step(state, inp):
        u_c, w_c, qk_c, qd_c, kd_c, gt_c = inp
        v_new = u_c - jnp.einsum('bhck,bhkv->bhcv', w_c, state)
        o_c = jnp.einsum('bhck,bhkv->bhcv', qd_c, state) + jnp.einsum('bhce,bhev->bhcv', qk_c, v_new)
        state = state * gt_c[..., None, None] + jnp.einsum('bhck,bhcv->bhkv', kd_c, v_new)
        return state, o_c

    xs = tuple(jnp.moveaxis(t, 1, 0) for t in (u, w, qk, q_dec, k_dec, g_tot))
    s0 = jnp.zeros((bsz, GDN_HEADS, GDN_DK, GDN_DV), f32)
    _, o = lax.scan(step, s0, xs)
    o = o.transpose(1, 0, 3, 2, 4).reshape(bsz, seq, GDN_HEADS, GDN_DV)
    zh = z.astype(f32).reshape(bsz, seq, GDN_HEADS, GDN_DV)
    o = rmsnorm(o, norm_w) * jax.nn.silu(zh)
    return o.reshape(bsz, seq, GDN_V)


def ssd_scan(xs, bm, cm, dt_raw, a_log, dt_bias, d_skip):
    f32 = jnp.float32
    bsz, seq, _ = xs.shape
    nc = seq // CHUNK
    hpg = SSD_HEADS // SSD_GROUPS
    x = xs.astype(f32).reshape(bsz, nc, CHUNK, SSD_GROUPS, hpg, SSD_HEADDIM)
    bm = bm.astype(f32).reshape(bsz, nc, CHUNK, SSD_GROUPS, SSD_STATE)
    cm = cm.astype(f32).reshape(bsz, nc, CHUNK, SSD_GROUPS, SSD_STATE)
    dt = jax.nn.softplus(dt_raw.astype(f32) + dt_bias.astype(f32)).reshape(bsz, nc, CHUNK, SSD_GROUPS, hpg)
    a = -jnp.exp(a_log.astype(f32)).reshape(SSD_GROUPS, hpg)
    acum = jnp.cumsum(dt * a, axis=2)
    xdt = x * dt[..., None]
    causal = jnp.tril(jnp.ones((CHUNK, CHUNK), bool))
    seg = acum[:, :, :, None] - acum[:, :, None, :]
    lmat = jnp.exp(jnp.where(causal[:, :, None, None], seg, -jnp.inf))
    cb = jnp.einsum('bncgs,bnegs->bnceg', cm, bm)
    y_diag = jnp.einsum('bnceg,bncegh,bneghp->bncghp', cb, lmat, xdt)
    decay_end = jnp.exp(acum[:, :, -1:] - acum)
    chunk_states = jnp.einsum('bncgs,bncgh,bncghp->bnghps', bm, decay_end, xdt)
    chunk_decay = jnp.exp(acum[:, :, -1])

    def step(state, inp):
        st, dc = inp
        return state * dc[..., None, None] + st, state

    s0 = jnp.zeros((bsz, SSD_GROUPS, hpg, SSD_HEADDIM, SSD_STATE), f32)
    _, prev = lax.scan(step, s0, (jnp.moveaxis(chunk_states, 1, 0), jnp.moveaxis(chunk_decay, 1, 0)))
    prev = jnp.moveaxis(prev, 0, 1)
    y_off = jnp.einsum('bncgs,bnghps,bncgh->bncghp', cm, prev, jnp.exp(acum))
    y = y_diag + y_off + d_skip.astype(f32).reshape(SSD_GROUPS, hpg)[:, :, None] * x
    return y.reshape(bsz, seq, SSD_INNER)


def rg_lru(x, w_a, b_a, w_x, b_x, lam):
    f32 = jnp.float32
    bsz, seq, _ = x.shape
    xf = x.astype(f32)
    xb = xf.reshape(bsz, seq, LRU_BLOCKS, LRU_BLOCK)
    r = jax.nn.sigmoid(jnp.einsum('bsnd,nde->bsne', xb, w_a.astype(f32)).reshape(bsz, seq, LRU_WIDTH) + b_a.astype(f32))
    i = jax.nn.sigmoid(jnp.einsum('bsnd,nde->bsne', xb, w_x.astype(f32)).reshape(bsz, seq, LRU_WIDTH) + b_x.astype(f32))
    log_a = -LRU_C * r * jax.nn.softplus(-lam.astype(f32))
    a = jnp.exp(log_a)
    u = jnp.sqrt(-jnp.expm1(2.0 * log_a)) * (i * xf)

    def combine(left, right):
        a_l, u_l = left
        a_r, u_r = right
        return a_l * a_r, a_r * u_l + u_r

    _, hs = lax.associative_scan(combine, (a, u), axis=1)
    return hs


def hybrid_mixer(h, w_in, gdn_conv_w, gdn_a_log, gdn_dt_bias, gdn_norm,
                 ssd_conv_w, ssd_conv_b, ssd_a_log, ssd_dt_bias, ssd_d, ssd_norm,
                 lru_conv_w, lru_conv_b, lru_w_a, lru_b_a, lru_w_x, lru_b_x, lru_lambda,
                 w_branch, w_out):
    bsz, seq, _ = h.shape
    f32 = jnp.float32
    proj = h @ w_in
    (q, k, v, gdn_z, gdn_b, gdn_a, ssd_x, ssd_z, ssd_bm, ssd_cm, ssd_dt,
     lru_x, lru_gate, gate_logits) = split_cols(proj, IN_SPLITS)
    qkv = jax.nn.silu(causal_conv(jnp.concatenate([q, k, v], axis=-1), gdn_conv_w))
    q, k, v = split_cols(qkv, (GDN_QK, GDN_QK, GDN_V))
    y_a = gated_deltanet(q, k, v, gdn_z, gdn_b, gdn_a, gdn_a_log, gdn_dt_bias, gdn_norm)
    xbc = jax.nn.silu(causal_conv(jnp.concatenate([ssd_x, ssd_bm, ssd_cm], axis=-1), ssd_conv_w) + ssd_conv_b)
    sx, sb, sc = split_cols(xbc, (SSD_INNER, SSD_BC, SSD_BC))
    y = ssd_scan(sx, sb, sc, ssd_dt, ssd_a_log, ssd_dt_bias, ssd_d)
    gz = (y * jax.nn.silu(ssd_z.astype(f32))).reshape(bsz, seq, SSD_GROUPS, SSD_INNER // SSD_GROUPS)
    y_b = rmsnorm(gz, ssd_norm.reshape(SSD_GROUPS, SSD_INNER // SSD_GROUPS)).reshape(bsz, seq, SSD_INNER)
    xc = causal_conv(lru_x, lru_conv_w) + lru_conv_b
    y_c = rg_lru(xc, lru_w_a, lru_b_a, lru_w_x, lru_b_x, lru_lambda) * jax.nn.gelu(lru_gate.astype(f32))
    gates = jax.nn.sigmoid(gate_logits.reshape(bsz, seq, N_BRANCH, D_MODEL))
    merged = sum(gates[:, :, r] * (y_r.astype(h.dtype) @ w_branch[r]) for r, y_r in enumerate((y_a, y_b, y_c)))
    return merged @ w_out


def setup_inputs(seed: int = 0) -> dict:
    key = jax.random.key(seed)
    ks = iter(jax.random.split(key, 40))
    L = DEPTH

    def nrm(shape, scale):
        return jax.random.normal(next(ks), shape, jnp.float32) * scale

    def gain(shape):
        return 1.0 + nrm(shape, 0.1)

    def dt_bias_init(n):
        dt = jnp.exp(jax.random.uniform(next(ks), (L, n), jnp.float32, math.log(1e-3), math.log(1e-1)))
        return dt + jnp.log(-jnp.expm1(-dt))

    def a_log_init(n):
        return jnp.log(jax.random.uniform(next(ks), (L, n), jnp.float32, 1.0, 16.0))

    x = nrm((BATCH, SEQ, D_MODEL), 1.0)
    c = nrm((BATCH, D_MODEL), 1.0)
    ada_w = nrm((L, D_MODEL, N_MOD * D_MODEL), 0.3 * D_MODEL ** -0.5)
    ada_b = nrm((L, N_MOD * D_MODEL), 0.02)
    norm_mix = gain((L, D_MODEL))
    w_in = nrm((L, D_MODEL, D_IN), D_MODEL ** -0.5)
    gdn_conv_w = nrm((L, CONV_WIDTH, 2 * GDN_QK + GDN_V), CONV_WIDTH ** -0.5)
    gdn_a_log = a_log_init(GDN_HEADS)
    gdn_dt_bias = dt_bias_init(GDN_HEADS)
    gdn_norm = gain((L, GDN_DV))
    ssd_conv_w = nrm((L, CONV_WIDTH, SSD_INNER + 2 * SSD_BC), CONV_WIDTH ** -0.5)
    ssd_conv_b = nrm((L, SSD_INNER + 2 * SSD_BC), 0.02)
    ssd_a_log = a_log_init(SSD_HEADS)
    ssd_dt_bias = dt_bias_init(SSD_HEADS)
    ssd_d = gain((L, SSD_HEADS))
    ssd_norm = gain((L, SSD_INNER))
    lru_conv_w = nrm((L, CONV_WIDTH, LRU_WIDTH), CONV_WIDTH ** -0.5)
    lru_conv_b = nrm((L, LRU_WIDTH), 0.02)
    lru_w_a = nrm((L, LRU_BLOCKS, LRU_BLOCK, LRU_BLOCK), LRU_BLOCK ** -0.5)
    lru_b_a = nrm((L, LRU_WIDTH), 0.02)
    lru_w_x = nrm((L, LRU_BLOCKS, LRU_BLOCK, LRU_BLOCK), LRU_BLOCK ** -0.5)
    lru_b_x = nrm((L, LRU_WIDTH), 0.02)
    a_pow = jax.random.uniform(next(ks), (L, LRU_WIDTH), jnp.float32, 0.9, 0.999)
    s = a_pow ** (1.0 / LRU_C)
    lru_lambda = jnp.log(s) - jnp.log1p(-s)
    w_branch = nrm((L, N_BRANCH, BRANCH_WIDTH, D_MODEL), BRANCH_WIDTH ** -0.5)
    w_out = nrm((L, D_MODEL, D_MODEL), D_MODEL ** -0.5)
    norm_mlp = gain((L, D_MODEL))
    w_up = nrm((L, D_MODEL, D_FF), D_MODEL ** -0.5)
    w_down = nrm((L, D_FF, D_MODEL), D_FF ** -0.5)
    final_norm = gain((D_MODEL,))
    return {'x': x, 'c': c, 'ada_w': ada_w, 'ada_b': ada_b, 'norm_mix': norm_mix, 'w_in': w_in,
            'gdn_conv_w': gdn_conv_w, 'gdn_a_log': gdn_a_log, 'gdn_dt_bias': gdn_dt_bias, 'gdn_norm': gdn_norm,
            'ssd_conv_w': ssd_conv_w, 'ssd_conv_b': ssd_conv_b, 'ssd_a_log': ssd_a_log, 'ssd_dt_bias': ssd_dt_bias,
            'ssd_d': ssd_d, 'ssd_norm': ssd_norm,
            'lru_conv_w': lru_conv_w, 'lru_conv_b': lru_conv_b, 'lru_w_a': lru_w_a, 'lru_b_a': lru_b_a,
            'lru_w_x': lru_w_x, 'lru_b_x': lru_b_x, 'lru_lambda': lru_lambda,
            'w_branch': w_branch, 'w_out': w_out, 'norm_mlp': norm_mlp, 'w_up': w_up, 'w_down': w_down,
            'final_norm': final_norm}


def reference(x, c, ada_w, ada_b, norm_mix, w_in, gdn_conv_w, gdn_a_log, gdn_dt_bias, gdn_norm,
              ssd_conv_w, ssd_conv_b, ssd_a_log, ssd_dt_bias, ssd_d, ssd_norm,
              lru_conv_w, lru_conv_b, lru_w_a, lru_b_a, lru_w_x, lru_b_x, lru_lambda,
              w_branch, w_out, norm_mlp, w_up, w_down, final_norm):
    for l in range(DEPTH):
        mod = jax.nn.silu(c) @ ada_w[l] + ada_b[l]
        sh1, sc1, gt1, sh2, sc2, gt2 = jnp.split(mod[:, None, :], N_MOD, axis=-1)
        h = rmsnorm(x, norm_mix[l]) * (1 + sc1) + sh1
        mix = hybrid_mixer(h, w_in[l], gdn_conv_w[l], gdn_a_log[l], gdn_dt_bias[l], gdn_norm[l],
                           ssd_conv_w[l], ssd_conv_b[l], ssd_a_log[l], ssd_dt_bias[l], ssd_d[l], ssd_norm[l],
                           lru_conv_w[l], lru_conv_b[l], lru_w_a[l], lru_b_a[l], lru_w_x[l], lru_b_x[l],
                           lru_lambda[l], w_branch[l], w_out[l])
        x = x + gt1 * mix
        h = rmsnorm(x, norm_mlp[l]) * (1 + sc2) + sh2
        x = x + gt2 * (jnp.square(jax.nn.relu(h @ w_up[l])) @ w_down[l])
    return rmsnorm(x, final_norm)
```

```python
import functools

import numpy as np
import jax
import jax.numpy as jnp
from jax import lax
from jax.experimental import pallas as pl
from jax.experimental.pallas import tpu as pltpu

F32 = jnp.float32
BF16 = jnp.bfloat16
HIGHEST = lax.Precision.HIGHEST

RMS_EPS = 1e-6
CONV_WIDTH = 4
LRU_C = 8.0
N_MOD = 6
N_BRANCH = 3

V7X_LANES = 128
V7X_SUBLANES = 8
SCAN_CHUNK = 128
INV_BASE = 16
SSD_GROUPS = 2
SMALL_WIDTH = V7X_LANES
VMEM_LIMIT = 56 * 1024 * 1024

_NT = (((1,), (1,)), ((), ()))
_TN = (((0,), (0,)), ((), ()))


def _dotb(a, b):
    return jnp.dot(a.astype(BF16), b.astype(BF16), preferred_element_type=F32)


def _softplus(x):
    return jnp.maximum(x, 0.0) + jnp.log1p(jnp.exp(-jnp.abs(x)))


def _silu(x):
    return x * jax.nn.sigmoid(x)


def _gelu_tanh(x):
    return 0.5 * x * (1.0 + jnp.tanh(np.sqrt(2.0 / np.pi).astype(np.float32) * (x + 0.044715 * (x * x * x))))


def _rms(x):
    return x * lax.rsqrt(jnp.mean(x * x, axis=-1, keepdims=True) + RMS_EPS)


def _resident(shape):
    nd = len(shape)
    return pl.BlockSpec(shape, lambda *_: (0,) * nd, pipeline_mode=pl.Buffered(1))


def _params(*sem):
    return pltpu.CompilerParams(dimension_semantics=sem, vmem_limit_bytes=VMEM_LIMIT)


def _mod_kernel(c_ref, w_ref, b_ref, o_ref):
    c = c_ref[...]
    o_ref[...] = jnp.dot(_silu(c), w_ref[...], precision=HIGHEST, preferred_element_type=F32) + b_ref[...]


def _modulation(c, ada_w, ada_b):
    depth, d, n = ada_w.shape
    bsz = c.shape[0]
    rows = -(-bsz // V7X_SUBLANES) * V7X_SUBLANES
    c_pad = jnp.pad(c, ((0, rows - bsz), (0, 0)))
    tn = n // 4
    out = pl.pallas_call(
        _mod_kernel,
        grid=(depth, n // tn),
        in_specs=[pl.BlockSpec((rows, d), lambda l, j: (0, 0)),
                  pl.BlockSpec((None, d, tn), lambda l, j: (l, 0, j)),
                  pl.BlockSpec((None, 1, tn), lambda l, j: (l, 0, j))],
        out_specs=pl.BlockSpec((None, rows, tn), lambda l, j: (l, 0, j)),
        out_shape=jax.ShapeDtypeStruct((depth, rows, n), F32),
        compiler_params=_params("parallel", "parallel"),
        name="adaln_mod",
    )(c_pad, ada_w, ada_b.reshape(depth, 1, n))
    return out[:, :bsz]


def _inproj_kernel(x_ref, nw_ref, sc_ref, sh_ref, w_ref, *o_refs, widths, nt):
    h = (_rms(x_ref[...]) * nw_ref[...] * (1.0 + sc_ref[...]) + sh_ref[...]).astype(BF16)
    off = 0
    for o_ref, width in zip(o_refs, widths):
        for c0 in range(0, width, nt):
            cw = min(nt, width - c0)
            acc = jnp.dot(h, w_ref[:, off + c0:off + c0 + cw], preferred_element_type=F32)
            o_ref[:, c0:c0 + cw] = acc.astype(o_ref.dtype)
        off += width


def _inproj(x, nw, sc, sh, w_all, widths, dtypes, tm):
    bsz, seq, d = x.shape
    tok = lambda width: pl.BlockSpec((None, tm, width), lambda b, i: (b, i, 0))
    per_batch = pl.BlockSpec((None, 1, d), lambda b, i: (b, 0, 0))
    return pl.pallas_call(
        functools.partial(_inproj_kernel, widths=widths, nt=512),
        grid=(bsz, seq // tm),
        in_specs=[tok(d), pl.BlockSpec((1, d), lambda b, i: (0, 0)), per_batch, per_batch,
                  _resident(w_all.shape)],
        out_specs=[tok(w) for w in widths],
        out_shape=[jax.ShapeDtypeStruct((bsz, seq, w), dt) for w, dt in zip(widths, dtypes)],
        compiler_params=_params("parallel", "parallel"),
        name="inproj",
    )(x, nw, sc, sh, w_all)


def _causal_conv(xin_ref, xs_ref, cw_ref):
    n = xin_ref.shape[0]
    first = pl.program_id(1) == 0

    @pl.when(first)
    def _():
        xs_ref[0:V7X_SUBLANES, :] = jnp.zeros((V7X_SUBLANES, xs_ref.shape[1]), F32)

    @pl.when(jnp.logical_not(first))
    def _():
        xs_ref[0:V7X_SUBLANES, :] = xs_ref[n:n + V7X_SUBLANES, :]

    xs_ref[V7X_SUBLANES:V7X_SUBLANES + n, :] = xin_ref[...].astype(F32)
    base = V7X_SUBLANES - (CONV_WIDTH - 1)
    acc = cw_ref[0:1, :] * xs_ref[pl.ds(base, n), :]
    for k in range(1, CONV_WIDTH):
        acc = acc + cw_ref[k:k + 1, :] * xs_ref[pl.ds(base + k, n), :]
    return acc


def _tri_masks(n):
    row = lax.broadcasted_iota(jnp.int32, (n, n), 0)
    col = lax.broadcasted_iota(jnp.int32, (n, n), 1)
    return row >= col, row > col, row == col


def _masked_exp_diff(mask, col_vec, row_vec):
    return jnp.where(mask, jnp.exp(jnp.where(mask, col_vec - row_vec, 0.0)), 0.0)


def _gdn_kernel(xin_ref, z_ref, s_ref, cw_ref, alog_ref, dtb_ref, nw_ref, y_ref, xs_ref, st_ref,
                *, nheads, dk, dv):
    n = xin_ref.shape[0]

    @pl.when(pl.program_id(1) == 0)
    def _():
        st_ref[...] = jnp.zeros(st_ref.shape, F32)

    qkv = _silu(_causal_conv(xin_ref, xs_ref, cw_ref))

    causal, strict, diag = _tri_masks(n)
    sv = s_ref[...]
    beta_all = jax.nn.sigmoid(sv)
    g_all = -jnp.exp(alog_ref[...]) * _softplus(sv + dtb_ref[...])
    gcum_all = jnp.dot(causal.astype(F32), g_all, precision=HIGHEST, preferred_element_type=F32)
    gcum_t = gcum_all.T
    eye = diag.astype(F32)
    row_i = lax.broadcasted_iota(jnp.int32, (n, n), 0)
    col_i = lax.broadcasted_iota(jnp.int32, (n, n), 1)
    blk_masks = [row_i // INV_BASE == col_i // INV_BASE]
    size = INV_BASE
    while size < n:
        blk_masks.append(jnp.logical_and(row_i // (2 * size) == col_i // (2 * size),
                                         jnp.logical_and((row_i // size) % 2 == 1, (col_i // size) % 2 == 0)))
        size *= 2

    for h in range(nheads):
        q = qkv[:, h * dk:(h + 1) * dk]
        k = qkv[:, (nheads + h) * dk:(nheads + h + 1) * dk]
        v = qkv[:, 2 * nheads * dk + h * dv:2 * nheads * dk + (h + 1) * dv]
        q = q * lax.rsqrt(jnp.sum(q * q, axis=-1, keepdims=True) + RMS_EPS) * (dk ** -0.5)
        k = k * lax.rsqrt(jnp.sum(k * k, axis=-1, keepdims=True) + RMS_EPS)
        beta = beta_all[:, h:h + 1]
        gc = gcum_all[:, nheads + h:nheads + h + 1]
        gr = gcum_t[nheads + h:nheads + h + 1, :]
        glast = gcum_all[n - 1:n, nheads + h:nheads + h + 1]
        dec = _masked_exp_diff(causal, gc, gr)

        kb = k.astype(BF16)
        kk = lax.dot_general(kb, kb, _NT, preferred_element_type=F32)
        m = jnp.where(strict, beta * kk * dec, 0.0)
        nm = -jnp.where(blk_masks[0], m, 0.0)
        inv = eye + nm
        for _ in range(int(np.log2(INV_BASE)) - 1):
            nm = _dotb(nm, nm)
            inv = inv + _dotb(inv, nm)
        for pair_mask in blk_masks[1:]:
            inv = inv - _dotb(inv, _dotb(jnp.where(pair_mask, m, 0.0), inv))

        egc = jnp.exp(gc)
        rhs = jnp.concatenate([beta * v, (beta * egc) * k], axis=1)
        sol = _dotb(inv, rhs)
        u = sol[:, :dv]
        w = sol[:, dv:]
        qk = lax.dot_general(q.astype(BF16), kb, _NT, preferred_element_type=F32) * dec
        q_dec = q * egc
        k_dec = k * jnp.exp(glast - gc)

        state = st_ref[h]
        sb = state.astype(BF16)
        v_new = u - jnp.dot(w.astype(BF16), sb, preferred_element_type=F32)
        vb = v_new.astype(BF16)
        o = (jnp.dot(q_dec.astype(BF16), sb, preferred_element_type=F32)
             + jnp.dot(qk.astype(BF16), vb, preferred_element_type=F32))
        st_ref[h] = state * jnp.exp(glast) + lax.dot_general(k_dec.astype(BF16), vb, _TN,
                                                             preferred_element_type=F32)

        z = z_ref[:, h * dv:(h + 1) * dv].astype(F32)
        y_ref[:, h * dv:(h + 1) * dv] = (_rms(o) * nw_ref[...] * _silu(z)).astype(y_ref.dtype)


def _gdn(p_gdn, p_z, p_s, conv_w, alog_row, dtb_row, norm_w, nheads, dk, dv):
    bsz, seq, width = p_gdn.shape
    n = SCAN_CHUNK
    vw = nheads * dv
    row = lambda wd: pl.BlockSpec((1, wd), lambda b, i: (0, 0))
    return pl.pallas_call(
        functools.partial(_gdn_kernel, nheads=nheads, dk=dk, dv=dv),
        grid=(bsz, seq // n),
        in_specs=[pl.BlockSpec((None, n, width), lambda b, i: (b, i, 0)),
                  pl.BlockSpec((None, n, vw), lambda b, i: (b, i, 0)),
                  pl.BlockSpec((None, n, SMALL_WIDTH), lambda b, i: (b, i, 0)),
                  pl.BlockSpec((CONV_WIDTH, width), lambda b, i: (0, 0)),
                  row(SMALL_WIDTH), row(SMALL_WIDTH), row(dv)],
        out_specs=pl.BlockSpec((None, n, vw), lambda b, i: (b, i, 0)),
        out_shape=jax.ShapeDtypeStruct((bsz, seq, vw), BF16),
        scratch_shapes=[pltpu.VMEM((n + V7X_SUBLANES, width), F32),
                        pltpu.VMEM((nheads, dk, dv), F32)],
        compiler_params=_params("parallel", "arbitrary"),
        name="gdn_mixer",
    )(p_gdn, p_z, p_s, conv_w, alog_row, dtb_row, norm_w)


def _ssd_kernel(xin_ref, z_ref, s_ref, cw_ref, cb_ref, alog_ref, dtb_ref, dexp_ref, nw_ref, e_ref,
                y_ref, xs_ref, st_ref, yb_ref, *, nheads, hd, ngroups, nstate, lane0):
    n = xin_ref.shape[0]
    inner = nheads * hd
    hpg = nheads // ngroups
    gw = hpg * hd

    @pl.when(pl.program_id(1) == 0)
    def _():
        st_ref[...] = jnp.zeros(st_ref.shape, F32)

    xbc = _silu(_causal_conv(xin_ref, xs_ref, cw_ref) + cb_ref[...])
    x = xbc[:, :inner]
    bm = xbc[:, inner:inner + ngroups * nstate]
    cm = xbc[:, inner + ngroups * nstate:]

    causal, _, _ = _tri_masks(n)
    lane = lax.broadcasted_iota(jnp.int32, (1, SMALL_WIDTH), 1)
    valid = jnp.logical_and(lane >= lane0, lane < lane0 + nheads)
    dt_all = jnp.where(valid, _softplus(s_ref[...] + dtb_ref[...]), 0.0)
    da = dt_all * -jnp.exp(alog_ref[...])
    acum_all = jnp.dot(causal.astype(F32), da, precision=HIGHEST, preferred_element_type=F32)
    acum_t = acum_all.T
    alast = acum_all[n - 1:n, :]

    expand = e_ref[...]
    widen = lambda t: jnp.dot(t.astype(BF16), expand, preferred_element_type=F32)
    dt_e = widen(dt_all)
    ea_e = widen(jnp.exp(acum_all))
    de_e = widen(jnp.exp(alast - acum_all))
    cd_e = jnp.dot(jnp.broadcast_to(jnp.exp(alast), (V7X_SUBLANES, SMALL_WIDTH)), expand.astype(F32),
                   precision=HIGHEST, preferred_element_type=F32)[0:1, :]

    xdt = x * dt_e
    xdt_b = xdt.astype(BF16)
    xw_b = (xdt * de_e).astype(BF16)
    for g in range(ngroups):
        bg = bm[:, g * nstate:(g + 1) * nstate].astype(BF16)
        cg = cm[:, g * nstate:(g + 1) * nstate].astype(BF16)
        cb = lax.dot_general(cg, bg, _NT, preferred_element_type=F32)
        for hh in range(hpg):
            h = g * hpg + hh
            ln = lane0 + h
            lmat = _masked_exp_diff(causal, acum_all[:, ln:ln + 1], acum_t[ln:ln + 1, :])
            yb_ref[:, h * hd:(h + 1) * hd] = jnp.dot((cb * lmat).astype(BF16), xdt_b[:, h * hd:(h + 1) * hd],
                                                     preferred_element_type=F32)
        sl = slice(g * gw, (g + 1) * gw)
        state = st_ref[g]
        y_off = jnp.dot(cg, state.astype(BF16), preferred_element_type=F32) * ea_e[:, sl]
        yb_ref[:, sl] = yb_ref[:, sl] + y_off
        st_ref[g] = state * cd_e[:, sl] + lax.dot_general(bg, xw_b[:, sl], _TN, preferred_element_type=F32)

    y = yb_ref[...] + dexp_ref[...] * x
    gz = y * _silu(z_ref[...].astype(F32))
    for g in range(ngroups):
        sl = slice(g * gw, (g + 1) * gw)
        y_ref[:, sl] = (_rms(gz[:, sl]) * nw_ref[:, sl]).astype(y_ref.dtype)


def _ssd(p_ssd, p_z, p_s, conv_w, conv_b, alog_row, dtb_row, d_exp, norm_w, expand, nheads, hd, ngroups,
         nstate, lane0):
    bsz, seq, width = p_ssd.shape
    n = SCAN_CHUNK
    inner = nheads * hd
    row = lambda wd: pl.BlockSpec((1, wd), lambda b, i: (0, 0))
    return pl.pallas_call(
        functools.partial(_ssd_kernel, nheads=nheads, hd=hd, ngroups=ngroups, nstate=nstate, lane0=lane0),
        grid=(bsz, seq // n),
        in_specs=[pl.BlockSpec((None, n, width), lambda b, i: (b, i, 0)),
                  pl.BlockSpec((None, n, inner), lambda b, i: (b, i, 1)),
                  pl.BlockSpec((None, n, SMALL_WIDTH), lambda b, i: (b, i, 0)),
                  pl.BlockSpec((CONV_WIDTH, width), lambda b, i: (0, 0)),
                  row(width), row(SMALL_WIDTH), row(SMALL_WIDTH), row(inner), row(inner),
                  pl.BlockSpec((SMALL_WIDTH, inner), lambda b, i: (0, 0))],
        out_specs=pl.BlockSpec((None, n, inner), lambda b, i: (b, i, 0)),
        out_shape=jax.ShapeDtypeStruct((bsz, seq, inner), BF16),
        scratch_shapes=[pltpu.VMEM((n + V7X_SUBLANES, width), F32),
                        pltpu.VMEM((ngroups, nstate, inner // ngroups), F32),
                        pltpu.VMEM((n, inner), F32)],
        compiler_params=_params("parallel", "arbitrary"),
        name="ssd_mixer",
    )(p_ssd, p_z, p_s, conv_w, conv_b, alog_row, dtb_row, d_exp, norm_w, expand)


def _lru_kernel(xin_ref, gate_ref, cw_ref, cb_ref, wax_ref, bax_ref, lam_ref, y_ref, xs_ref, h_ref):
    n, width = xin_ref.shape

    @pl.when(pl.program_id(1) == 0)
    def _():
        h_ref[...] = jnp.zeros(h_ref.shape, F32)

    xc = _causal_conv(xin_ref, xs_ref, cw_ref) + cb_ref[...]
    rx = jnp.dot(xc.astype(BF16), wax_ref[...], preferred_element_type=F32) + bax_ref[...]
    r = jax.nn.sigmoid(rx[:, :width])
    ig = jax.nn.sigmoid(rx[:, width:])
    a = jnp.exp(-LRU_C * r * _softplus(-lam_ref[...]))
    u = jnp.sqrt(1.0 - a * a) * (ig * xc)

    rows = lax.broadcasted_iota(jnp.int32, (n, width), 0)
    d = 1
    while d < n:
        keep = rows >= d
        u = jnp.where(keep, a * pltpu.roll(u, d, 0) + u, u)
        a = jnp.where(keep, a * pltpu.roll(a, d, 0), a)
        d *= 2
    hs = a * h_ref[...] + u
    h_ref[...] = hs[n - 1:n, :]
    y_ref[...] = (hs * _gelu_tanh(gate_ref[...].astype(F32))).astype(y_ref.dtype)


def _lru(p_lru, p_z, conv_w, conv_b, wax, bax, lam, tl):
    bsz, seq, width = p_lru.shape
    row = lambda wd: pl.BlockSpec((1, wd), lambda b, i: (0, 0))
    return pl.pallas_call(
        _lru_kernel,
        grid=(bsz, seq // tl),
        in_specs=[pl.BlockSpec((None, tl, width), lambda b, i: (b, i, 0)),
                  pl.BlockSpec((None, tl, width), lambda b, i: (b, i, 2)),
                  pl.BlockSpec((CONV_WIDTH, width), lambda b, i: (0, 0)),
                  row(width), pl.BlockSpec((width, 2 * width), lambda b, i: (0, 0)), row(2 * width), row(width)],
        out_specs=pl.BlockSpec((None, tl, width), lambda b, i: (b, i, 0)),
        out_shape=jax.ShapeDtypeStruct((bsz, seq, width), BF16),
        scratch_shapes=[pltpu.VMEM((tl + V7X_SUBLANES, width), F32),
                        pltpu.VMEM((1, width), F32)],
        compiler_params=_params("parallel", "arbitrary"),
        name="lru_mixer",
    )(p_lru, p_z, conv_w, conv_b, wax, bax, lam)


def _merge_mlp_kernel(ya_ref, yb_ref, yc_ref, gl_ref, x_ref, gt1_ref, nw_ref, sc_ref, sh_ref, gt2_ref,
                      wb_ref, wo_ref, wup_ref, wdn_ref, fnw_ref, o_ref, *, final, fc):
    d = x_ref.shape[1]
    merged = None
    for r, y_ref in enumerate((ya_ref, yb_ref, yc_ref)):
        br = jnp.dot(y_ref[...], wb_ref[r], preferred_element_type=F32)
        term = jax.nn.sigmoid(gl_ref[:, r * d:(r + 1) * d].astype(F32)) * br
        merged = term if merged is None else merged + term
    mix = jnp.dot(merged.astype(BF16), wo_ref[...], preferred_element_type=F32)
    x1 = x_ref[...] + gt1_ref[...] * mix

    h = (_rms(x1) * nw_ref[...] * (1.0 + sc_ref[...]) + sh_ref[...]).astype(BF16)
    acc = None
    for c0 in range(0, wup_ref.shape[1], fc):
        act = jnp.maximum(jnp.dot(h, wup_ref[:, c0:c0 + fc], preferred_element_type=F32), 0.0)
        part = jnp.dot((act * act).astype(BF16), wdn_ref[c0:c0 + fc, :], preferred_element_type=F32)
        acc = part if acc is None else acc + part
    x2 = x1 + gt2_ref[...] * acc
    if final:
        x2 = _rms(x2) * fnw_ref[...]
    o_ref[...] = x2


def _merge_mlp(ya, yb, yc, p_gl, x, gt1, nw, sc, sh, gt2, wb, wo, wup, wdn, fnw, final, tm):
    bsz, seq, d = x.shape
    bw = ya.shape[2]
    tok = lambda width: pl.BlockSpec((None, tm, width), lambda b, i: (b, i, 0))
    per_batch = pl.BlockSpec((None, 1, d), lambda b, i: (b, 0, 0))
    row = pl.BlockSpec((1, d), lambda b, i: (0, 0))
    return pl.pallas_call(
        functools.partial(_merge_mlp_kernel, final=final, fc=1024),
        grid=(bsz, seq // tm),
        in_specs=[tok(bw), tok(bw), tok(bw), tok(N_BRANCH * d), tok(d), per_batch, row, per_batch, per_batch,
                  per_batch, _resident(wb.shape), _resident(wo.shape), _resident(wup.shape),
                  _resident(wdn.shape), row],
        out_specs=tok(d),
        out_shape=jax.ShapeDtypeStruct((bsz, seq, d), F32),
        compiler_params=_params("parallel", "parallel"),
        name="merge_mlp",
    )(ya, yb, yc, p_gl, x, gt1, nw, sc, sh, gt2, wb, wo, wup, wdn, fnw)


def _lane_row(vec, lane0):
    return jnp.pad(vec.astype(F32), (lane0, SMALL_WIDTH - lane0 - vec.shape[0])).reshape(1, SMALL_WIDTH)


def _block_diag(w):
    nb, a, b = w.shape
    eye = jnp.eye(nb, dtype=w.dtype)
    return (eye[:, None, :, None] * w[:, :, None, :]).reshape(nb * a, nb * b)


def kernel(x, c, ada_w, ada_b, norm_mix, w_in, gdn_conv_w, gdn_a_log, gdn_dt_bias, gdn_norm, ssd_conv_w, ssd_conv_b, ssd_a_log, ssd_dt_bias, ssd_d, ssd_norm, lru_conv_w, lru_conv_b, lru_w_a, lru_b_a, lru_w_x, lru_b_x, lru_lambda, w_branch, w_out, norm_mlp, w_up, w_down, final_norm):
    bsz, seq, d = x.shape
    depth = ada_w.shape[0]
    gh = gdn_a_log.shape[1]
    dv = gdn_norm.shape[1]
    gqk = (gdn_conv_w.shape[2] - gh * dv) // 2
    dk = gqk // gh
    sh_ = ssd_a_log.shape[1]
    inner = ssd_norm.shape[1]
    hd = inner // sh_
    sbc = (ssd_conv_w.shape[2] - inner) // 2
    ngroups = SSD_GROUPS
    nstate = sbc // ngroups
    lw = lru_lambda.shape[1]
    assert seq % SCAN_CHUNK == 0 and 2 * gh + sh_ <= SMALL_WIDTH

    splits = (gqk, gqk, gh * dv, gh * dv, gh, gh, inner, inner, sbc, sbc, sh_, lw, lw, N_BRANCH * d)
    offs = np.concatenate([[0], np.cumsum(splits)])
    cols = lambda w, i: w[:, offs[i]:offs[i + 1]]
    ssd_lane0 = 2 * gh
    widths = (2 * gqk + gh * dv, inner + 2 * sbc, lw, gh * dv + inner + lw, N_BRANCH * d, SMALL_WIDTH)
    dtypes = (BF16, BF16, BF16, BF16, BF16, F32)

    mod = _modulation(c, ada_w, ada_b)
    expand = jnp.asarray(np.kron(np.eye(SMALL_WIDTH, sh_, -ssd_lane0), np.ones((1, hd))), BF16)

    tm = 512
    for l in range(depth):
        sh1, sc1, gt1, sh2, sc2, gt2 = [mod[l, :, None, j * d:(j + 1) * d] for j in range(N_MOD)]
        w = w_in[l]
        small = jnp.concatenate([cols(w, 4), cols(w, 5), cols(w, 10)], axis=1)
        w_all = jnp.concatenate(
            [cols(w, 0), cols(w, 1), cols(w, 2),
             cols(w, 6), cols(w, 8), cols(w, 9),
             cols(w, 11),
             cols(w, 3), cols(w, 7), cols(w, 12),
             cols(w, 13),
             jnp.pad(small, ((0, 0), (0, SMALL_WIDTH - small.shape[1])))], axis=1).astype(BF16)
        p_gdn, p_ssd, p_lru, p_z, p_gl, p_s = _inproj(
            x, norm_mix[l].reshape(1, d), sc1, sh1, w_all, widths, dtypes, tm)

        y_a = _gdn(p_gdn, p_z, p_s, gdn_conv_w[l], _lane_row(gdn_a_log[l], gh), _lane_row(gdn_dt_bias[l], gh),
                   gdn_norm[l].reshape(1, dv), gh, dk, dv)
        y_b = _ssd(p_ssd, p_z, p_s, ssd_conv_w[l], ssd_conv_b[l].reshape(1, -1),
                   _lane_row(ssd_a_log[l], ssd_lane0), _lane_row(ssd_dt_bias[l], ssd_lane0),
                   jnp.repeat(ssd_d[l], hd).reshape(1, inner), ssd_norm[l].reshape(1, inner), expand,
                   sh_, hd, ngroups, nstate, ssd_lane0)
        wax = jnp.concatenate([_block_diag(lru_w_a[l]), _block_diag(lru_w_x[l])], axis=1).astype(BF16)
        bax = jnp.concatenate([lru_b_a[l], lru_b_x[l]]).reshape(1, 2 * lw)
        y_c = _lru(p_lru, p_z, lru_conv_w[l], lru_conv_b[l].reshape(1, lw), wax, bax,
                   lru_lambda[l].reshape(1, lw), 256)

        x = _merge_mlp(y_a, y_b, y_c, p_gl, x, gt1, norm_mlp[l].reshape(1, d), sc2, sh2, gt2,
                       w_branch[l].astype(BF16), w_out[l].astype(BF16), w_up[l].astype(BF16),
                       w_down[l].astype(BF16), final_norm.reshape(1, d), l == depth - 1, tm)
    return x
```

```python
import functools

import numpy as np
import jax
import jax.numpy as jnp
from jax import lax
from jax.experimental import pallas as pl
from jax.experimental.pallas import tpu as pltpu

F32 = jnp.float32
BF16 = jnp.bfloat16
HIGHEST = lax.Precision.HIGHEST

RMS_EPS = 1e-6
CONV_WIDTH = 4
LRU_C = 8.0
N_MOD = 6
N_BRANCH = 3

V7X_LANES = 128
V7X_SUBLANES = 8
SCAN_CHUNK = 128
GDN_BLOCK = 512
INV_BASE = 16
SSD_GROUPS = 2
SMALL_WIDTH = V7X_LANES
VMEM_LIMIT = 56 * 1024 * 1024

_NT = (((1,), (1,)), ((), ()))
_TN = (((0,), (0,)), ((), ()))


def _dotb(a, b):
    return jnp.dot(a.astype(BF16), b.astype(BF16), preferred_element_type=F32)


def _softplus(x):
    return jnp.maximum(x, 0.0) + jnp.log1p(jnp.exp(-jnp.abs(x)))


def _silu(x):
    return x * jax.nn.sigmoid(x)


def _gelu_tanh(x):
    return 0.5 * x * (1.0 + jnp.tanh(np.sqrt(2.0 / np.pi).astype(np.float32) * (x + 0.044715 * (x * x * x))))


def _rms(x):
    return x * lax.rsqrt(jnp.mean(x * x, axis=-1, keepdims=True) + RMS_EPS)


def _resident(shape):
    nd = len(shape)
    return pl.BlockSpec(shape, lambda *_: (0,) * nd, pipeline_mode=pl.Buffered(1))


def _params(*sem):
    return pltpu.CompilerParams(dimension_semantics=sem, vmem_limit_bytes=VMEM_LIMIT)


def _mod_kernel(c_ref, w_ref, b_ref, o_ref):
    c = c_ref[...]
    o_ref[...] = jnp.dot(_silu(c), w_ref[...], precision=HIGHEST, preferred_element_type=F32) + b_ref[...]


def _modulation(c, ada_w, ada_b):
    depth, d, n = ada_w.shape
    bsz = c.shape[0]
    rows = -(-bsz // V7X_SUBLANES) * V7X_SUBLANES
    c_pad = jnp.pad(c, ((0, rows - bsz), (0, 0)))
    tn = n // 4
    out = pl.pallas_call(
        _mod_kernel,
        grid=(depth, n // tn),
        in_specs=[pl.BlockSpec((rows, d), lambda l, j: (0, 0)),
                  pl.BlockSpec((None, d, tn), lambda l, j: (l, 0, j)),
                  pl.BlockSpec((None, 1, tn), lambda l, j: (l, 0, j))],
        out_specs=pl.BlockSpec((None, rows, tn), lambda l, j: (l, 0, j)),
        out_shape=jax.ShapeDtypeStruct((depth, rows, n), F32),
        compiler_params=_params("parallel", "parallel"),
        name="adaln_mod",
    )(c_pad, ada_w, ada_b.reshape(depth, 1, n))
    return out[:, :bsz]


def _inproj_kernel(x_ref, nw_ref, sc_ref, sh_ref, w_ref, *o_refs, widths, nt):
    h = (_rms(x_ref[...]) * nw_ref[...] * (1.0 + sc_ref[...]) + sh_ref[...]).astype(BF16)
    off = 0
    for o_ref, width in zip(o_refs, widths):
        for c0 in range(0, width, nt):
            cw = min(nt, width - c0)
            acc = jnp.dot(h, w_ref[:, off + c0:off + c0 + cw], preferred_element_type=F32)
            o_ref[:, c0:c0 + cw] = acc.astype(o_ref.dtype)
        off += width


def _inproj(x, nw, sc, sh, w_all, widths, dtypes, tm):
    bsz, seq, d = x.shape
    tok = lambda width: pl.BlockSpec((None, tm, width), lambda b, i: (b, i, 0))
    per_batch = pl.BlockSpec((None, 1, d), lambda b, i: (b, 0, 0))
    return pl.pallas_call(
        functools.partial(_inproj_kernel, widths=widths, nt=512),
        grid=(bsz, seq // tm),
        in_specs=[tok(d), pl.BlockSpec((1, d), lambda b, i: (0, 0)), per_batch, per_batch,
                  _resident(w_all.shape)],
        out_specs=[tok(w) for w in widths],
        out_shape=[jax.ShapeDtypeStruct((bsz, seq, w), dt) for w, dt in zip(widths, dtypes)],
        compiler_params=_params("parallel", "parallel"),
        name="inproj",
    )(x, nw, sc, sh, w_all)


def _causal_conv(xin_ref, xs_ref, cw_ref):
    n = xin_ref.shape[0]
    first = pl.program_id(1) == 0

    @pl.when(first)
    def _():
        xs_ref[0:V7X_SUBLANES, :] = jnp.zeros((V7X_SUBLANES, xs_ref.shape[1]), F32)

    @pl.when(jnp.logical_not(first))
    def _():
        xs_ref[0:V7X_SUBLANES, :] = xs_ref[n:n + V7X_SUBLANES, :]

    xs_ref[V7X_SUBLANES:V7X_SUBLANES + n, :] = xin_ref[...].astype(F32)
    base = V7X_SUBLANES - (CONV_WIDTH - 1)
    acc = cw_ref[0:1, :] * xs_ref[pl.ds(base, n), :]
    for k in range(1, CONV_WIDTH):
        acc = acc + cw_ref[k:k + 1, :] * xs_ref[pl.ds(base + k, n), :]
    return acc


def _tri_masks(n):
    row = lax.broadcasted_iota(jnp.int32, (n, n), 0)
    col = lax.broadcasted_iota(jnp.int32, (n, n), 1)
    return row >= col, row > col, row == col


def _masked_exp_diff(mask, col_vec, row_vec):
    return jnp.where(mask, jnp.exp(jnp.where(mask, col_vec - row_vec, 0.0)), 0.0)


def _round_robin(chains):
    active = list(chains)
    while active:
        still = []
        for chain in active:
            try:
                next(chain)
                still.append(chain)
            except StopIteration:
                pass
        active = still


def _dot_exact01(a01, x):
    x1 = x.astype(BF16)
    r1 = x - x1.astype(F32)
    x2 = r1.astype(BF16)
    x3 = (r1 - x2.astype(F32)).astype(BF16)
    return (jnp.dot(a01, x1, preferred_element_type=F32) + jnp.dot(a01, x2, preferred_element_type=F32)
            + jnp.dot(a01, x3, preferred_element_type=F32))


def _inverse_masks(n):
    row_i = lax.broadcasted_iota(jnp.int32, (n, n), 0)
    col_i = lax.broadcasted_iota(jnp.int32, (n, n), 1)
    masks = [row_i // INV_BASE == col_i // INV_BASE]
    size = INV_BASE
    while size < n:
        masks.append(jnp.logical_and(row_i // (2 * size) == col_i // (2 * size),
                                     jnp.logical_and((row_i // size) % 2 == 1, (col_i // size) % 2 == 0)))
        size *= 2
    return masks


def _gdn_kernel(xin_ref, z_ref, s_ref, cw_ref, alog_ref, dtb_ref, nw_ref, y_ref, xs_ref, st_ref, qkv_ref,
                *, nheads, dk, dv):
    n = SCAN_CHUNK
    nchunks = xin_ref.shape[0] // n

    @pl.when(pl.program_id(1) == 0)
    def _():
        st_ref[...] = jnp.zeros(st_ref.shape, F32)

    qkv_ref[...] = _silu(_causal_conv(xin_ref, xs_ref, cw_ref))

    causal, strict, diag = _tri_masks(n)
    tril_b = causal.astype(BF16)
    eye = diag.astype(F32)
    blk_masks = _inverse_masks(n)
    a_neg = -jnp.exp(alog_ref[...])
    dtb = dtb_ref[...]

    beta_c, gcum_c, gcum_t = [], [], []
    for c in range(nchunks):
        sv = s_ref[c * n:(c + 1) * n, :]
        beta_c.append(jax.nn.sigmoid(sv))
        gcum_c.append(_dot_exact01(tril_b, a_neg * _softplus(sv + dtb)))
        gcum_t.append(gcum_c[-1].T)

    res = {}

    def chain(c, h):
        rows = slice(c * n, (c + 1) * n)
        q = qkv_ref[rows, h * dk:(h + 1) * dk]
        k = qkv_ref[rows, (nheads + h) * dk:(nheads + h + 1) * dk]
        v = qkv_ref[rows, 2 * nheads * dk + h * dv:2 * nheads * dk + (h + 1) * dv]
        q = q * lax.rsqrt(jnp.sum(q * q, axis=-1, keepdims=True) + RMS_EPS) * (dk ** -0.5)
        k = k * lax.rsqrt(jnp.sum(k * k, axis=-1, keepdims=True) + RMS_EPS)
        beta = beta_c[c][:, h:h + 1]
        gc = gcum_c[c][:, nheads + h:nheads + h + 1]
        gr = gcum_t[c][nheads + h:nheads + h + 1, :]
        glast = gcum_c[c][n - 1:n, nheads + h:nheads + h + 1]
        dec = _masked_exp_diff(causal, gc, gr)
        kb = k.astype(BF16)
        kk = lax.dot_general(kb, kb, _NT, preferred_element_type=F32)
        qk = lax.dot_general(q.astype(BF16), kb, _NT, preferred_element_type=F32) * dec
        yield
        m = jnp.where(strict, beta * kk * dec, 0.0)
        nm = -jnp.where(blk_masks[0], m, 0.0)
        inv = eye + nm
        for _ in range(int(np.log2(INV_BASE)) - 1):
            nm_b = nm.astype(BF16)
            nm = jnp.dot(nm_b, nm_b, preferred_element_type=F32)
            yield
            inv = inv + _dotb(inv, nm)
            yield
        for pair_mask in blk_masks[1:]:
            inv_b = inv.astype(BF16)
            low = jnp.dot(jnp.where(pair_mask, m, 0.0).astype(BF16), inv_b, preferred_element_type=F32)
            yield
            inv = inv - jnp.dot(inv_b, low.astype(BF16), preferred_element_type=F32)
            yield
        egc = jnp.exp(gc)
        rhs = jnp.concatenate([beta * v, (beta * egc) * k], axis=1)
        uw = _dotb(inv, rhs).astype(BF16)
        yield
        k_uw = lax.dot_general((k * jnp.exp(glast - gc)).astype(BF16), uw, _TN, preferred_element_type=F32)
        q_uw = jnp.dot(qk.astype(BF16), uw, preferred_element_type=F32)
        yield
        lhs = jnp.concatenate([-k_uw[:, dv:], q * egc - q_uw[:, dv:]], axis=0).astype(BF16)
        res[c, h] = (lhs, k_uw[:, :dv], q_uw[:, :dv], jnp.exp(glast))

    _round_robin([chain(c, h) for c in range(nchunks) for h in range(nheads)])

    states = [st_ref[h] for h in range(nheads)]
    outs = {}
    for c in range(nchunks):
        for h in range(nheads):
            lhs, k_u, q_u, g_tot = res[c, h]
            prod = jnp.dot(lhs, states[h].astype(BF16), preferred_element_type=F32)
            outs[c, h] = prod[dk:] + q_u
            states[h] = states[h] * g_tot + prod[:dk] + k_u
    for h in range(nheads):
        st_ref[h] = states[h]

    for c in range(nchunks):
        rows = slice(c * n, (c + 1) * n)
        for h in range(nheads):
            z = z_ref[rows, h * dv:(h + 1) * dv].astype(F32)
            y_ref[rows, h * dv:(h + 1) * dv] = (_rms(outs[c, h]) * nw_ref[...] * _silu(z)).astype(y_ref.dtype)


def _gdn(p_gdn, p_z, p_s, conv_w, alog_row, dtb_row, norm_w, nheads, dk, dv, tl):
    bsz, seq, width = p_gdn.shape
    vw = nheads * dv
    row = lambda wd: pl.BlockSpec((1, wd), lambda b, i: (0, 0))
    return pl.pallas_call(
        functools.partial(_gdn_kernel, nheads=nheads, dk=dk, dv=dv),
        grid=(bsz, seq // tl),
        in_specs=[pl.BlockSpec((None, tl, width), lambda b, i: (b, i, 0)),
                  pl.BlockSpec((None, tl, vw), lambda b, i: (b, i, 0)),
                  pl.BlockSpec((None, tl, SMALL_WIDTH), lambda b, i: (b, i, 0)),
                  pl.BlockSpec((CONV_WIDTH, width), lambda b, i: (0, 0)),
                  row(SMALL_WIDTH), row(SMALL_WIDTH), row(dv)],
        out_specs=pl.BlockSpec((None, tl, vw), lambda b, i: (b, i, 0)),
        out_shape=jax.ShapeDtypeStruct((bsz, seq, vw), BF16),
        scratch_shapes=[pltpu.VMEM((tl + V7X_SUBLANES, width), F32),
                        pltpu.VMEM((nheads, dk, dv), F32),
                        pltpu.VMEM((tl, width), F32)],
        compiler_params=_params("parallel", "arbitrary"),
        name="gdn_mixer",
    )(p_gdn, p_z, p_s, conv_w, alog_row, dtb_row, norm_w)


def _ssd_kernel(xin_ref, z_ref, s_ref, cw_ref, cb_ref, alog_ref, dtb_ref, dexp_ref, nw_ref, e_ref,
                y_ref, xs_ref, st_ref, yb_ref, *, nheads, hd, ngroups, nstate, lane0):
    n = xin_ref.shape[0]
    inner = nheads * hd
    hpg = nheads // ngroups
    gw = hpg * hd

    @pl.when(pl.program_id(1) == 0)
    def _():
        st_ref[...] = jnp.zeros(st_ref.shape, F32)

    xbc = _silu(_causal_conv(xin_ref, xs_ref, cw_ref) + cb_ref[...])
    x = xbc[:, :inner]
    bm = xbc[:, inner:inner + ngroups * nstate]
    cm = xbc[:, inner + ngroups * nstate:]

    causal, _, _ = _tri_masks(n)
    lane = lax.broadcasted_iota(jnp.int32, (1, SMALL_WIDTH), 1)
    valid = jnp.logical_and(lane >= lane0, lane < lane0 + nheads)
    dt_all = jnp.where(valid, _softplus(s_ref[...] + dtb_ref[...]), 0.0)
    da = dt_all * -jnp.exp(alog_ref[...])
    acum_all = jnp.dot(causal.astype(F32), da, precision=HIGHEST, preferred_element_type=F32)
    acum_t = acum_all.T
    alast = acum_all[n - 1:n, :]

    expand = e_ref[...]
    widen = lambda t: jnp.dot(t.astype(BF16), expand, preferred_element_type=F32)
    dt_e = widen(dt_all)
    ea_e = widen(jnp.exp(acum_all))
    de_e = widen(jnp.exp(alast - acum_all))
    cd_e = jnp.dot(jnp.broadcast_to(jnp.exp(alast), (V7X_SUBLANES, SMALL_WIDTH)), expand.astype(F32),
                   precision=HIGHEST, preferred_element_type=F32)[0:1, :]

    xdt = x * dt_e
    xdt_b = xdt.astype(BF16)
    xw_b = (xdt * de_e).astype(BF16)
    for g in range(ngroups):
        bg = bm[:, g * nstate:(g + 1) * nstate].astype(BF16)
        cg = cm[:, g * nstate:(g + 1) * nstate].astype(BF16)
        cb = lax.dot_general(cg, bg, _NT, preferred_element_type=F32)
        for hh in range(hpg):
            h = g * hpg + hh
            ln = lane0 + h
            lmat = _masked_exp_diff(causal, acum_all[:, ln:ln + 1], acum_t[ln:ln + 1, :])
            yb_ref[:, h * hd:(h + 1) * hd] = jnp.dot((cb * lmat).astype(BF16), xdt_b[:, h * hd:(h + 1) * hd],
                                                     preferred_element_type=F32)
        sl = slice(g * gw, (g + 1) * gw)
        state = st_ref[g]
        y_off = jnp.dot(cg, state.astype(BF16), preferred_element_type=F32) * ea_e[:, sl]
        yb_ref[:, sl] = yb_ref[:, sl] + y_off
        st_ref[g] = state * cd_e[:, sl] + lax.dot_general(bg, xw_b[:, sl], _TN, preferred_element_type=F32)

    y = yb_ref[...] + dexp_ref[...] * x
    gz = y * _silu(z_ref[...].astype(F32))
    for g in range(ngroups):
        sl = slice(g * gw, (g + 1) * gw)
        y_ref[:, sl] = (_rms(gz[:, sl]) * nw_ref[:, sl]).astype(y_ref.dtype)


def _ssd(p_ssd, p_z, p_s, conv_w, conv_b, alog_row, dtb_row, d_exp, norm_w, expand, nheads, hd, ngroups,
         nstate, lane0):
    bsz, seq, width = p_ssd.shape
    n = SCAN_CHUNK
    inner = nheads * hd
    row = lambda wd: pl.BlockSpec((1, wd), lambda b, i: (0, 0))
    return pl.pallas_call(
        functools.partial(_ssd_kernel, nheads=nheads, hd=hd, ngroups=ngroups, nstate=nstate, lane0=lane0),
        grid=(bsz, seq // n),
        in_specs=[pl.BlockSpec((None, n, width), lambda b, i: (b, i, 0)),
                  pl.BlockSpec((None, n, inner), lambda b, i: (b, i, 1)),
                  pl.BlockSpec((None, n, SMALL_WIDTH), lambda b, i: (b, i, 0)),
                  pl.BlockSpec((CONV_WIDTH, width), lambda b, i: (0, 0)),
                  row(width), row(SMALL_WIDTH), row(SMALL_WIDTH), row(inner), row(inner),
                  pl.BlockSpec((SMALL_WIDTH, inner), lambda b, i: (0, 0))],
        out_specs=pl.BlockSpec((None, n, inner), lambda b, i: (b, i, 0)),
        out_shape=jax.ShapeDtypeStruct((bsz, seq, inner), BF16),
        scratch_shapes=[pltpu.VMEM((n + V7X_SUBLANES, width), F32),
                        pltpu.VMEM((ngroups, nstate, inner // ngroups), F32),
                        pltpu.VMEM((n, inner), F32)],
        compiler_params=_params("parallel", "arbitrary"),
        name="ssd_mixer",
    )(p_ssd, p_z, p_s, conv_w, conv_b, alog_row, dtb_row, d_exp, norm_w, expand)


def _lru_kernel(xin_ref, gate_ref, cw_ref, cb_ref, wax_ref, bax_ref, lam_ref, y_ref, xs_ref, h_ref):
    n, width = xin_ref.shape

    @pl.when(pl.program_id(1) == 0)
    def _():
        h_ref[...] = jnp.zeros(h_ref.shape, F32)

    xc = _causal_conv(xin_ref, xs_ref, cw_ref) + cb_ref[...]
    rx = jnp.dot(xc.astype(BF16), wax_ref[...], preferred_element_type=F32) + bax_ref[...]
    r = jax.nn.sigmoid(rx[:, :width])
    ig = jax.nn.sigmoid(rx[:, width:])
    a = jnp.exp(-LRU_C * r * _softplus(-lam_ref[...]))
    u = jnp.sqrt(1.0 - a * a) * (ig * xc)

    rows = lax.broadcasted_iota(jnp.int32, (n, width), 0)
    d = 1
    while d < n:
        keep = rows >= d
        u = jnp.where(keep, a * pltpu.roll(u, d, 0) + u, u)
        a = jnp.where(keep, a * pltpu.roll(a, d, 0), a)
        d *= 2
    hs = a * h_ref[...] + u
    h_ref[...] = hs[n - 1:n, :]
    y_ref[...] = (hs * _gelu_tanh(gate_ref[...].astype(F32))).astype(y_ref.dtype)


def _lru(p_lru, p_z, conv_w, conv_b, wax, bax, lam, tl):
    bsz, seq, width = p_lru.shape
    row = lambda wd: pl.BlockSpec((1, wd), lambda b, i: (0, 0))
    return pl.pallas_call(
        _lru_kernel,
        grid=(bsz, seq // tl),
        in_specs=[pl.BlockSpec((None, tl, width), lambda b, i: (b, i, 0)),
                  pl.BlockSpec((None, tl, width), lambda b, i: (b, i, 2)),
                  pl.BlockSpec((CONV_WIDTH, width), lambda b, i: (0, 0)),
                  row(width), pl.BlockSpec((width, 2 * width), lambda b, i: (0, 0)), row(2 * width), row(width)],
        out_specs=pl.BlockSpec((None, tl, width), lambda b, i: (b, i, 0)),
        out_shape=jax.ShapeDtypeStruct((bsz, seq, width), BF16),
        scratch_shapes=[pltpu.VMEM((tl + V7X_SUBLANES, width), F32),
                        pltpu.VMEM((1, width), F32)],
        compiler_params=_params("parallel", "arbitrary"),
        name="lru_mixer",
    )(p_lru, p_z, conv_w, conv_b, wax, bax, lam)


def _merge_mlp_kernel(ya_ref, yb_ref, yc_ref, gl_ref, x_ref, gt1_ref, nw_ref, sc_ref, sh_ref, gt2_ref,
                      wb_ref, wo_ref, wup_ref, wdn_ref, fnw_ref, o_ref, *, final, fc):
    d = x_ref.shape[1]
    merged = None
    for r, y_ref in enumerate((ya_ref, yb_ref, yc_ref)):
        br = jnp.dot(y_ref[...], wb_ref[r], preferred_element_type=F32)
        term = jax.nn.sigmoid(gl_ref[:, r * d:(r + 1) * d].astype(F32)) * br
        merged = term if merged is None else merged + term
    mix = jnp.dot(merged.astype(BF16), wo_ref[...], preferred_element_type=F32)
    x1 = x_ref[...] + gt1_ref[...] * mix

    h = (_rms(x1) * nw_ref[...] * (1.0 + sc_ref[...]) + sh_ref[...]).astype(BF16)
    acc = None
    for c0 in range(0, wup_ref.shape[1], fc):
        act = jnp.maximum(jnp.dot(h, wup_ref[:, c0:c0 + fc], preferred_element_type=F32), 0.0)
        part = jnp.dot((act * act).astype(BF16), wdn_ref[c0:c0 + fc, :], preferred_element_type=F32)
        acc = part if acc is None else acc + part
    x2 = x1 + gt2_ref[...] * acc
    if final:
        x2 = _rms(x2) * fnw_ref[...]
    o_ref[...] = x2


def _merge_mlp(ya, yb, yc, p_gl, x, gt1, nw, sc, sh, gt2, wb, wo, wup, wdn, fnw, final, tm):
    bsz, seq, d = x.shape
    bw = ya.shape[2]
    tok = lambda width: pl.BlockSpec((None, tm, width), lambda b, i: (b, i, 0))
    per_batch = pl.BlockSpec((None, 1, d), lambda b, i: (b, 0, 0))
    row = pl.BlockSpec((1, d), lambda b, i: (0, 0))
    return pl.pallas_call(
        functools.partial(_merge_mlp_kernel, final=final, fc=1024),
        grid=(bsz, seq // tm),
        in_specs=[tok(bw), tok(bw), tok(bw), tok(N_BRANCH * d), tok(d), per_batch, row, per_batch, per_batch,
                  per_batch, _resident(wb.shape), _resident(wo.shape), _resident(wup.shape),
                  _resident(wdn.shape), row],
        out_specs=tok(d),
        out_shape=jax.ShapeDtypeStruct((bsz, seq, d), F32),
        compiler_params=_params("parallel", "parallel"),
        name="merge_mlp",
    )(ya, yb, yc, p_gl, x, gt1, nw, sc, sh, gt2, wb, wo, wup, wdn, fnw)


def _lane_row(vec, lane0):
    return jnp.pad(vec.astype(F32), (lane0, SMALL_WIDTH - lane0 - vec.shape[0])).reshape(1, SMALL_WIDTH)


def _block_diag(w):
    nb, a, b = w.shape
    eye = jnp.eye(nb, dtype=w.dtype)
    return (eye[:, None, :, None] * w[:, :, None, :]).reshape(nb * a, nb * b)


def kernel(x, c, ada_w, ada_b, norm_mix, w_in, gdn_conv_w, gdn_a_log, gdn_dt_bias, gdn_norm, ssd_conv_w, ssd_conv_b, ssd_a_log, ssd_dt_bias, ssd_d, ssd_norm, lru_conv_w, lru_conv_b, lru_w_a, lru_b_a, lru_w_x, lru_b_x, lru_lambda, w_branch, w_out, norm_mlp, w_up, w_down, final_norm):
    bsz, seq, d = x.shape
    depth = ada_w.shape[0]
    gh = gdn_a_log.shape[1]
    dv = gdn_norm.shape[1]
    gqk = (gdn_conv_w.shape[2] - gh * dv) // 2
    dk = gqk // gh
    sh_ = ssd_a_log.shape[1]
    inner = ssd_norm.shape[1]
    hd = inner // sh_
    sbc = (ssd_conv_w.shape[2] - inner) // 2
    ngroups = SSD_GROUPS
    nstate = sbc // ngroups
    lw = lru_lambda.shape[1]
    assert seq % GDN_BLOCK == 0 and GDN_BLOCK % SCAN_CHUNK == 0 and 2 * gh + sh_ <= SMALL_WIDTH

    splits = (gqk, gqk, gh * dv, gh * dv, gh, gh, inner, inner, sbc, sbc, sh_, lw, lw, N_BRANCH * d)
    offs = np.concatenate([[0], np.cumsum(splits)])
    cols = lambda w, i: w[:, offs[i]:offs[i + 1]]
    ssd_lane0 = 2 * gh
    widths = (2 * gqk + gh * dv, inner + 2 * sbc, lw, gh * dv + inner + lw, N_BRANCH * d, SMALL_WIDTH)
    dtypes = (BF16, BF16, BF16, BF16, BF16, F32)

    mod = _modulation(c, ada_w, ada_b)
    expand = jnp.asarray(np.kron(np.eye(SMALL_WIDTH, sh_, -ssd_lane0), np.ones((1, hd))), BF16)

    tm = 512
    for l in range(depth):
        sh1, sc1, gt1, sh2, sc2, gt2 = [mod[l, :, None, j * d:(j + 1) * d] for j in range(N_MOD)]
        w = w_in[l]
        small = jnp.concatenate([cols(w, 4), cols(w, 5), cols(w, 10)], axis=1)
        w_all = jnp.concatenate(
            [cols(w, 0), cols(w, 1), cols(w, 2),
             cols(w, 6), cols(w, 8), cols(w, 9),
             cols(w, 11),
             cols(w, 3), cols(w, 7), cols(w, 12),
             cols(w, 13),
             jnp.pad(small, ((0, 0), (0, SMALL_WIDTH - small.shape[1])))], axis=1).astype(BF16)
        p_gdn, p_ssd, p_lru, p_z, p_gl, p_s = _inproj(
            x, norm_mix[l].reshape(1, d), sc1, sh1, w_all, widths, dtypes, tm)

        y_a = _gdn(p_gdn, p_z, p_s, gdn_conv_w[l], _lane_row(gdn_a_log[l], gh), _lane_row(gdn_dt_bias[l], gh),
                   gdn_norm[l].reshape(1, dv), gh, dk, dv, GDN_BLOCK)
        y_b = _ssd(p_ssd, p_z, p_s, ssd_conv_w[l], ssd_conv_b[l].reshape(1, -1),
                   _lane_row(ssd_a_log[l], ssd_lane0), _lane_row(ssd_dt_bias[l], ssd_lane0),
                   jnp.repeat(ssd_d[l], hd).reshape(1, inner), ssd_norm[l].reshape(1, inner), expand,
                   sh_, hd, ngroups, nstate, ssd_lane0)
        wax = jnp.concatenate([_block_diag(lru_w_a[l]), _block_diag(lru_w_x[l])], axis=1).astype(BF16)
        bax = jnp.concatenate([lru_b_a[l], lru_b_x[l]]).reshape(1, 2 * lw)
        y_c = _lru(p_lru, p_z, lru_conv_w[l], lru_conv_b[l].reshape(1, lw), wax, bax,
                   lru_lambda[l].reshape(1, lw), 256)

        x = _merge_mlp(y_a, y_b, y_c, p_gl, x, gt1, norm_mlp[l].reshape(1, d), sc2, sh2, gt2,
                       w_branch[l].astype(BF16), w_out[l].astype(BF16), w_up[l].astype(BF16),
                       w_down[l].astype(BF16), final_norm.reshape(1, d), l == depth - 1, tm)
    return x
```

```python
import functools

import numpy as np
import jax
import jax.numpy as jnp
from jax import lax
from jax.experimental import pallas as pl
from jax.experimental.pallas import tpu as pltpu

F32 = jnp.float32
BF16 = jnp.bfloat16
HIGHEST = lax.Precision.HIGHEST

RMS_EPS = 1e-6
CONV_WIDTH = 4
LRU_C = 8.0
N_MOD = 6
N_BRANCH = 3

V7X_LANES = 128
V7X_SUBLANES = 8
SCAN_CHUNK = 128
MIXER_BLOCK = 512
INV_BASE = 16
SSD_GROUPS = 2
SMALL_WIDTH = V7X_LANES
VMEM_LIMIT = 56 * 1024 * 1024

_NT = (((1,), (1,)), ((), ()))
_TN = (((0,), (0,)), ((), ()))


def _dotb(a, b):
    return jnp.dot(a.astype(BF16), b.astype(BF16), preferred_element_type=F32)


def _softplus(x):
    return jnp.maximum(x, 0.0) + jnp.log1p(jnp.exp(-jnp.abs(x)))


def _silu(x):
    return x * jax.nn.sigmoid(x)


def _gelu_tanh(x):
    return 0.5 * x * (1.0 + jnp.tanh(np.sqrt(2.0 / np.pi).astype(np.float32) * (x + 0.044715 * (x * x * x))))


def _rms(x):
    return x * lax.rsqrt(jnp.mean(x * x, axis=-1, keepdims=True) + RMS_EPS)


def _resident(shape):
    nd = len(shape)
    return pl.BlockSpec(shape, lambda *_: (0,) * nd, pipeline_mode=pl.Buffered(1))


def _params(*sem):
    return pltpu.CompilerParams(dimension_semantics=sem, vmem_limit_bytes=VMEM_LIMIT)


def _mod_kernel(c_ref, w_ref, b_ref, o_ref):
    c = c_ref[...]
    o_ref[...] = jnp.dot(_silu(c), w_ref[...], precision=HIGHEST, preferred_element_type=F32) + b_ref[...]


def _modulation(c, ada_w, ada_b):
    depth, d, n = ada_w.shape
    bsz = c.shape[0]
    rows = -(-bsz // V7X_SUBLANES) * V7X_SUBLANES
    c_pad = jnp.pad(c, ((0, rows - bsz), (0, 0)))
    tn = n // 4
    out = pl.pallas_call(
        _mod_kernel,
        grid=(depth, n // tn),
        in_specs=[pl.BlockSpec((rows, d), lambda l, j: (0, 0)),
                  pl.BlockSpec((None, d, tn), lambda l, j: (l, 0, j)),
                  pl.BlockSpec((None, 1, tn), lambda l, j: (l, 0, j))],
        out_specs=pl.BlockSpec((None, rows, tn), lambda l, j: (l, 0, j)),
        out_shape=jax.ShapeDtypeStruct((depth, rows, n), F32),
        compiler_params=_params("parallel", "parallel"),
        name="adaln_mod",
    )(c_pad, ada_w, ada_b.reshape(depth, 1, n))
    return out[:, :bsz]


def _inproj_kernel(x_ref, nw_ref, sc_ref, sh_ref, w_ref, cw_ref, cb_ref, *refs, pieces, nt):
    o_refs, stage_ref = refs[:len(pieces)], refs[len(pieces)]
    tm = x_ref.shape[0]
    tail = slice(tm, tm + V7X_SUBLANES)

    @pl.when(pl.program_id(1) == 0)
    def _():
        stage_ref[:, tail, :] = jnp.zeros((stage_ref.shape[0], V7X_SUBLANES, nt), F32)

    h = (_rms(x_ref[...]) * nw_ref[...] * (1.0 + sc_ref[...]) + sh_ref[...]).astype(BF16)

    def column_chunk(o_ref, c0, cw, off, coff, conv, bias, act):
        acc = jnp.dot(h, w_ref[:, off + c0:off + c0 + cw], preferred_element_type=F32)
        if conv:
            j = (coff + c0) // nt
            cs = slice(coff + c0, coff + c0 + cw)
            stage_ref[j, 0:V7X_SUBLANES, :] = stage_ref[j, tail, :]
            stage_ref[j, V7X_SUBLANES:V7X_SUBLANES + tm, :] = acc
            base = V7X_SUBLANES - (CONV_WIDTH - 1)
            acc = cw_ref[0:1, cs] * stage_ref[j, pl.ds(base, tm), :]
            for k in range(1, CONV_WIDTH):
                acc = acc + cw_ref[k:k + 1, cs] * stage_ref[j, pl.ds(base + k, tm), :]
            if bias:
                acc = acc + cb_ref[:, cs]
            if act:
                acc = _silu(acc)
        o_ref[:, c0:c0 + cw] = acc.astype(o_ref.dtype)

    heavy, plain = [], []
    off = coff = 0
    for o_ref, (width, conv, bias, act) in zip(o_refs, pieces):
        for c0 in range(0, width, nt):
            (heavy if conv else plain).append((o_ref, c0, min(nt, width - c0), off, coff, conv, bias, act))
        off += width
        coff += width if conv else 0
    while heavy or plain:
        for queue in (heavy, plain):
            if queue:
                column_chunk(*queue.pop(0))


def _inproj(x, nw, sc, sh, w_all, conv_w, conv_b, pieces, dtypes, tm):
    bsz, seq, d = x.shape
    nt = 512
    conv_width = conv_w.shape[1]
    assert all(p[0] % nt == 0 for p in pieces if p[1])
    tok = lambda width: pl.BlockSpec((None, tm, width), lambda b, i: (b, i, 0))
    per_batch = pl.BlockSpec((None, 1, d), lambda b, i: (b, 0, 0))
    return pl.pallas_call(
        functools.partial(_inproj_kernel, pieces=pieces, nt=nt),
        grid=(bsz, seq // tm),
        in_specs=[tok(d), pl.BlockSpec((1, d), lambda b, i: (0, 0)), per_batch, per_batch,
                  _resident(w_all.shape), pl.BlockSpec((CONV_WIDTH, conv_width), lambda b, i: (0, 0)),
                  pl.BlockSpec((1, conv_width), lambda b, i: (0, 0))],
        out_specs=[tok(p[0]) for p in pieces],
        out_shape=[jax.ShapeDtypeStruct((bsz, seq, p[0]), dt) for p, dt in zip(pieces, dtypes)],
        scratch_shapes=[pltpu.VMEM((conv_width // nt, tm + V7X_SUBLANES, nt), F32)],
        compiler_params=_params("parallel", "arbitrary"),
        name="inproj",
    )(x, nw, sc, sh, w_all, conv_w, conv_b)


def _tri_masks(n):
    row = lax.broadcasted_iota(jnp.int32, (n, n), 0)
    col = lax.broadcasted_iota(jnp.int32, (n, n), 1)
    return row >= col, row > col, row == col


def _masked_exp_diff(mask, col_vec, row_vec):
    return jnp.where(mask, jnp.exp(jnp.where(mask, col_vec - row_vec, 0.0)), 0.0)


def _round_robin(chains):
    active = list(chains)
    while active:
        still = []
        for chain in active:
            try:
                next(chain)
                still.append(chain)
            except StopIteration:
                pass
        active = still


def _dot_exact01(a01, x):
    x1 = x.astype(BF16)
    r1 = x - x1.astype(F32)
    x2 = r1.astype(BF16)
    x3 = (r1 - x2.astype(F32)).astype(BF16)
    return (jnp.dot(a01, x1, preferred_element_type=F32) + jnp.dot(a01, x2, preferred_element_type=F32)
            + jnp.dot(a01, x3, preferred_element_type=F32))


def _dot_exact01_rhs(x, b01):
    x1 = x.astype(BF16)
    r1 = x - x1.astype(F32)
    x2 = r1.astype(BF16)
    x3 = (r1 - x2.astype(F32)).astype(BF16)
    return (jnp.dot(x1, b01, preferred_element_type=F32) + jnp.dot(x2, b01, preferred_element_type=F32)
            + jnp.dot(x3, b01, preferred_element_type=F32))


def _inverse_masks(n):
    row_i = lax.broadcasted_iota(jnp.int32, (n, n), 0)
    col_i = lax.broadcasted_iota(jnp.int32, (n, n), 1)
    masks = [row_i // INV_BASE == col_i // INV_BASE]
    size = INV_BASE
    while size < n:
        masks.append(jnp.logical_and(row_i // (2 * size) == col_i // (2 * size),
                                     jnp.logical_and((row_i // size) % 2 == 1, (col_i // size) % 2 == 0)))
        size *= 2
    return masks


def _gdn_kernel(qkv_ref, z_ref, s_ref, alog_ref, dtb_ref, nw_ref, y_ref, st_ref, *, nheads, dk, dv):
    n = SCAN_CHUNK
    nchunks = qkv_ref.shape[0] // n

    @pl.when(pl.program_id(1) == 0)
    def _():
        st_ref[...] = jnp.zeros(st_ref.shape, F32)

    causal, strict, diag = _tri_masks(n)
    tril_b = causal.astype(BF16)
    eye = diag.astype(F32)
    blk_masks = _inverse_masks(n)
    a_neg = -jnp.exp(alog_ref[...])
    dtb = dtb_ref[...]

    beta_c, gcum_c, gcum_t = [], [], []
    for c in range(nchunks):
        sv = s_ref[c * n:(c + 1) * n, :]
        beta_c.append(jax.nn.sigmoid(sv))
        gcum_c.append(_dot_exact01(tril_b, a_neg * _softplus(sv + dtb)))
        gcum_t.append(gcum_c[-1].T)

    res = {}

    def chain(c, h):
        rows = slice(c * n, (c + 1) * n)
        q = qkv_ref[rows, h * dk:(h + 1) * dk].astype(F32)
        k = qkv_ref[rows, (nheads + h) * dk:(nheads + h + 1) * dk].astype(F32)
        v = qkv_ref[rows, 2 * nheads * dk + h * dv:2 * nheads * dk + (h + 1) * dv].astype(F32)
        q = q * lax.rsqrt(jnp.sum(q * q, axis=-1, keepdims=True) + RMS_EPS) * (dk ** -0.5)
        k = k * lax.rsqrt(jnp.sum(k * k, axis=-1, keepdims=True) + RMS_EPS)
        beta = beta_c[c][:, h:h + 1]
        gc = gcum_c[c][:, nheads + h:nheads + h + 1]
        gr = gcum_t[c][nheads + h:nheads + h + 1, :]
        glast = gcum_c[c][n - 1:n, nheads + h:nheads + h + 1]
        dec = _masked_exp_diff(causal, gc, gr)
        kb = k.astype(BF16)
        kk = lax.dot_general(kb, kb, _NT, preferred_element_type=F32)
        qk = lax.dot_general(q.astype(BF16), kb, _NT, preferred_element_type=F32) * dec
        yield
        m = jnp.where(strict, beta * kk * dec, 0.0)
        nm = -jnp.where(blk_masks[0], m, 0.0)
        inv = eye + nm
        for _ in range(int(np.log2(INV_BASE)) - 1):
            nm_b = nm.astype(BF16)
            nm = jnp.dot(nm_b, nm_b, preferred_element_type=F32)
            yield
            inv = inv + _dotb(inv, nm)
            yield
        for pair_mask in blk_masks[1:]:
            inv_b = inv.astype(BF16)
            low = jnp.dot(jnp.where(pair_mask, m, 0.0).astype(BF16), inv_b, preferred_element_type=F32)
            yield
            inv = inv - jnp.dot(inv_b, low.astype(BF16), preferred_element_type=F32)
            yield
        egc = jnp.exp(gc)
        rhs = jnp.concatenate([beta * v, (beta * egc) * k], axis=1)
        uw = _dotb(inv, rhs).astype(BF16)
        yield
        k_uw = lax.dot_general((k * jnp.exp(glast - gc)).astype(BF16), uw, _TN, preferred_element_type=F32)
        q_uw = jnp.dot(qk.astype(BF16), uw, preferred_element_type=F32)
        yield
        lhs = jnp.concatenate([-k_uw[:, dv:], q * egc - q_uw[:, dv:]], axis=0).astype(BF16)
        res[c, h] = (lhs, k_uw[:, :dv], q_uw[:, :dv], jnp.exp(glast))

    _round_robin([chain(c, h) for c in range(nchunks) for h in range(nheads)])

    states = [st_ref[h] for h in range(nheads)]
    outs = {}
    for c in range(nchunks):
        for h in range(nheads):
            lhs, k_u, q_u, g_tot = res[c, h]
            prod = jnp.dot(lhs, states[h].astype(BF16), preferred_element_type=F32)
            outs[c, h] = prod[dk:] + q_u
            states[h] = states[h] * g_tot + prod[:dk] + k_u
    for h in range(nheads):
        st_ref[h] = states[h]

    for c in range(nchunks):
        rows = slice(c * n, (c + 1) * n)
        for h in range(nheads):
            z = z_ref[rows, h * dv:(h + 1) * dv].astype(F32)
            y_ref[rows, h * dv:(h + 1) * dv] = (_rms(outs[c, h]) * nw_ref[...] * _silu(z)).astype(y_ref.dtype)


def _gdn(p_gdn, p_z, p_s, alog_row, dtb_row, norm_w, nheads, dk, dv, tl):
    bsz, seq, width = p_gdn.shape
    vw = nheads * dv
    row = lambda wd: pl.BlockSpec((1, wd), lambda b, i: (0, 0))
    return pl.pallas_call(
        functools.partial(_gdn_kernel, nheads=nheads, dk=dk, dv=dv),
        grid=(bsz, seq // tl),
        in_specs=[pl.BlockSpec((None, tl, width), lambda b, i: (b, i, 0)),
                  pl.BlockSpec((None, tl, vw), lambda b, i: (b, i, 0)),
                  pl.BlockSpec((None, tl, SMALL_WIDTH), lambda b, i: (b, i, 0)),
                  row(SMALL_WIDTH), row(SMALL_WIDTH), row(dv)],
        out_specs=pl.BlockSpec((None, tl, vw), lambda b, i: (b, i, 0)),
        out_shape=jax.ShapeDtypeStruct((bsz, seq, vw), BF16),
        scratch_shapes=[pltpu.VMEM((nheads, dk, dv), F32)],
        compiler_params=_params("parallel", "arbitrary"),
        name="gdn_mixer",
    )(p_gdn, p_z, p_s, alog_row, dtb_row, norm_w)


def _ssd_kernel(xin_ref, z_ref, s_ref, alog_ref, dtb_ref, dexp_ref, nw_ref, e_ref,
                y_ref, st_ref, yb_ref, *, nheads, hd, ngroups, nstate, lane0):
    n = SCAN_CHUNK
    nchunks = xin_ref.shape[0] // n
    inner = nheads * hd
    hpg = nheads // ngroups
    gw = hpg * hd

    @pl.when(pl.program_id(1) == 0)
    def _():
        st_ref[...] = jnp.zeros(st_ref.shape, F32)

    causal, _, _ = _tri_masks(n)
    tril_b = causal.astype(BF16)
    lane = lax.broadcasted_iota(jnp.int32, (1, SMALL_WIDTH), 1)
    valid = jnp.logical_and(lane >= lane0, lane < lane0 + nheads)
    a_neg = -jnp.exp(alog_ref[...])
    dtb = dtb_ref[...]
    expand = e_ref[...]
    widen = lambda t: jnp.dot(t.astype(BF16), expand, preferred_element_type=F32)

    acum_c, acum_t, ea_c, cd_c, xdt_c, xw_c = [], [], [], [], [], []
    for c in range(nchunks):
        rows = slice(c * n, (c + 1) * n)
        dt_all = jnp.where(valid, _softplus(s_ref[rows, :] + dtb), 0.0)
        acum = _dot_exact01(tril_b, dt_all * a_neg)
        alast = acum[n - 1:n, :]
        acum_c.append(acum)
        acum_t.append(acum.T)
        ea_c.append(widen(jnp.exp(acum)))
        cd_c.append(_dot_exact01_rhs(jnp.broadcast_to(jnp.exp(alast), (V7X_SUBLANES, SMALL_WIDTH)),
                                     expand)[0:1, :])
        xdt = xin_ref[rows, :inner].astype(F32) * widen(dt_all)
        xdt_c.append(xdt.astype(BF16))
        xw_c.append((xdt * widen(jnp.exp(alast - acum))).astype(BF16))

    grown = {}

    def chain(c, g):
        rows = slice(c * n, (c + 1) * n)
        sl = slice(g * gw, (g + 1) * gw)
        bg = xin_ref[rows, inner + g * nstate:inner + (g + 1) * nstate]
        cg = xin_ref[rows, inner + (ngroups + g) * nstate:inner + (ngroups + g + 1) * nstate]
        cb = lax.dot_general(cg, bg, _NT, preferred_element_type=F32)
        grown[c, g] = lax.dot_general(bg, xw_c[c][:, sl], _TN, preferred_element_type=F32)
        yield
        for hh in range(hpg):
            h = g * hpg + hh
            ln = lane0 + h
            lmat = _masked_exp_diff(causal, acum_c[c][:, ln:ln + 1], acum_t[c][ln:ln + 1, :])
            yb_ref[rows, h * hd:(h + 1) * hd] = jnp.dot((cb * lmat).astype(BF16), xdt_c[c][:, h * hd:(h + 1) * hd],
                                                        preferred_element_type=F32)
            yield

    _round_robin([chain(c, g) for c in range(nchunks) for g in range(ngroups)])

    for g in range(ngroups):
        sl = slice(g * gw, (g + 1) * gw)
        state = st_ref[g]
        for c in range(nchunks):
            rows = slice(c * n, (c + 1) * n)
            cg = xin_ref[rows, inner + (ngroups + g) * nstate:inner + (ngroups + g + 1) * nstate]
            y_off = jnp.dot(cg, state.astype(BF16), preferred_element_type=F32) * ea_c[c][:, sl]
            yb_ref[rows, sl] = yb_ref[rows, sl] + y_off
            state = state * cd_c[c][:, sl] + grown[c, g]
        st_ref[g] = state

    for c in range(nchunks):
        rows = slice(c * n, (c + 1) * n)
        y = yb_ref[rows, :] + dexp_ref[...] * xin_ref[rows, :inner].astype(F32)
        gz = y * _silu(z_ref[rows, :].astype(F32))
        for g in range(ngroups):
            sl = slice(g * gw, (g + 1) * gw)
            y_ref[rows, sl] = (_rms(gz[:, sl]) * nw_ref[:, sl]).astype(y_ref.dtype)


def _ssd(p_ssd, p_z, p_s, alog_row, dtb_row, d_exp, norm_w, expand, nheads, hd, ngroups, nstate, lane0, n):
    bsz, seq, width = p_ssd.shape
    inner = nheads * hd
    row = lambda wd: pl.BlockSpec((1, wd), lambda b, i: (0, 0))
    return pl.pallas_call(
        functools.partial(_ssd_kernel, nheads=nheads, hd=hd, ngroups=ngroups, nstate=nstate, lane0=lane0),
        grid=(bsz, seq // n),
        in_specs=[pl.BlockSpec((None, n, width), lambda b, i: (b, i, 0)),
                  pl.BlockSpec((None, n, inner), lambda b, i: (b, i, 1)),
                  pl.BlockSpec((None, n, SMALL_WIDTH), lambda b, i: (b, i, 0)),
                  row(SMALL_WIDTH), row(SMALL_WIDTH), row(inner), row(inner),
                  pl.BlockSpec((SMALL_WIDTH, inner), lambda b, i: (0, 0))],
        out_specs=pl.BlockSpec((None, n, inner), lambda b, i: (b, i, 0)),
        out_shape=jax.ShapeDtypeStruct((bsz, seq, inner), BF16),
        scratch_shapes=[pltpu.VMEM((ngroups, nstate, inner // ngroups), F32),
                        pltpu.VMEM((n, inner), F32)],
        compiler_params=_params("parallel", "arbitrary"),
        name="ssd_mixer",
    )(p_ssd, p_z, p_s, alog_row, dtb_row, d_exp, norm_w, expand)


def _lru_kernel(xin_ref, gate_ref, wax_ref, bax_ref, lam_ref, y_ref, h_ref):
    n, width = xin_ref.shape

    @pl.when(pl.program_id(1) == 0)
    def _():
        h_ref[...] = jnp.zeros(h_ref.shape, F32)

    xc = xin_ref[...].astype(F32)
    rx = jnp.dot(xin_ref[...], wax_ref[...], preferred_element_type=F32) + bax_ref[...]
    r = jax.nn.sigmoid(rx[:, :width])
    ig = jax.nn.sigmoid(rx[:, width:])
    a = jnp.exp(-LRU_C * r * _softplus(-lam_ref[...]))
    u = jnp.sqrt(1.0 - a * a) * (ig * xc)

    groups = n // V7X_SUBLANES
    a = a.reshape(groups, V7X_SUBLANES, width)
    u = u.reshape(groups, V7X_SUBLANES, width)
    sub = lax.broadcasted_iota(jnp.int32, (groups, V7X_SUBLANES, width), 1)
    d = 1
    while d < V7X_SUBLANES:
        keep = sub >= d
        u = jnp.where(keep, a * pltpu.roll(u, d, 1) + u, u)
        a = jnp.where(keep, a * pltpu.roll(a, d, 1), a)
        d *= 2
    h = h_ref[...]
    hs = []
    for g in range(groups):
        hs.append(a[g] * h + u[g])
        h = hs[-1][V7X_SUBLANES - 1:V7X_SUBLANES, :]
    h_ref[...] = h
    hs = jnp.concatenate(hs, axis=0)
    y_ref[...] = (hs * _gelu_tanh(gate_ref[...].astype(F32))).astype(y_ref.dtype)


def _lru(p_lru, p_z, wax, bax, lam, tl):
    bsz, seq, width = p_lru.shape
    row = lambda wd: pl.BlockSpec((1, wd), lambda b, i: (0, 0))
    return pl.pallas_call(
        _lru_kernel,
        grid=(bsz, seq // tl),
        in_specs=[pl.BlockSpec((None, tl, width), lambda b, i: (b, i, 0)),
                  pl.BlockSpec((None, tl, width), lambda b, i: (b, i, 2)),
                  pl.BlockSpec((width, 2 * width), lambda b, i: (0, 0)), row(2 * width), row(width)],
        out_specs=pl.BlockSpec((None, tl, width), lambda b, i: (b, i, 0)),
        out_shape=jax.ShapeDtypeStruct((bsz, seq, width), BF16),
        scratch_shapes=[pltpu.VMEM((1, width), F32)],
        compiler_params=_params("parallel", "arbitrary"),
        name="lru_mixer",
    )(p_lru, p_z, wax, bax, lam)


def _merge_mlp_kernel(ya_ref, yb_ref, yc_ref, gl_ref, x_ref, gt1_ref, nw_ref, sc_ref, sh_ref, gt2_ref,
                      wb_ref, wo_ref, wup_ref, wdn_ref, fnw_ref, o_ref, *, final, fc):
    d = x_ref.shape[1]
    merged = None
    for r, y_ref in enumerate((ya_ref, yb_ref, yc_ref)):
        br = jnp.dot(y_ref[...], wb_ref[r], preferred_element_type=F32)
        term = jax.nn.sigmoid(gl_ref[:, r * d:(r + 1) * d].astype(F32)) * br
        merged = term if merged is None else merged + term
    mix = jnp.dot(merged.astype(BF16), wo_ref[...], preferred_element_type=F32)
    x1 = x_ref[...] + gt1_ref[...] * mix

    h = (_rms(x1) * nw_ref[...] * (1.0 + sc_ref[...]) + sh_ref[...]).astype(BF16)
    acc = None
    for c0 in range(0, wup_ref.shape[1], fc):
        act = jnp.maximum(jnp.dot(h, wup_ref[:, c0:c0 + fc], preferred_element_type=F32), 0.0)
        part = jnp.dot((act * act).astype(BF16), wdn_ref[c0:c0 + fc, :], preferred_element_type=F32)
        acc = part if acc is None else acc + part
    x2 = x1 + gt2_ref[...] * acc
    if final:
        x2 = _rms(x2) * fnw_ref[...]
    o_ref[...] = x2


def _merge_mlp(ya, yb, yc, p_gl, x, gt1, nw, sc, sh, gt2, wb, wo, wup, wdn, fnw, final, tm):
    bsz, seq, d = x.shape
    bw = ya.shape[2]
    tok = lambda width: pl.BlockSpec((None, tm, width), lambda b, i: (b, i, 0))
    per_batch = pl.BlockSpec((None, 1, d), lambda b, i: (b, 0, 0))
    row = pl.BlockSpec((1, d), lambda b, i: (0, 0))
    return pl.pallas_call(
        functools.partial(_merge_mlp_kernel, final=final, fc=1024),
        grid=(bsz, seq // tm),
        in_specs=[tok(bw), tok(bw), tok(bw), tok(N_BRANCH * d), tok(d), per_batch, row, per_batch, per_batch,
                  per_batch, _resident(wb.shape), _resident(wo.shape), _resident(wup.shape),
                  _resident(wdn.shape), row],
        out_specs=tok(d),
        out_shape=jax.ShapeDtypeStruct((bsz, seq, d), F32),
        compiler_params=_params("parallel", "parallel"),
        name="merge_mlp",
    )(ya, yb, yc, p_gl, x, gt1, nw, sc, sh, gt2, wb, wo, wup, wdn, fnw)


def _lane_row(vec, lane0):
    return jnp.pad(vec.astype(F32), (lane0, SMALL_WIDTH - lane0 - vec.shape[0])).reshape(1, SMALL_WIDTH)


def _block_diag(w):
    nb, a, b = w.shape
    eye = jnp.eye(nb, dtype=w.dtype)
    return (eye[:, None, :, None] * w[:, :, None, :]).reshape(nb * a, nb * b)


def kernel(x, c, ada_w, ada_b, norm_mix, w_in, gdn_conv_w, gdn_a_log, gdn_dt_bias, gdn_norm, ssd_conv_w, ssd_conv_b, ssd_a_log, ssd_dt_bias, ssd_d, ssd_norm, lru_conv_w, lru_conv_b, lru_w_a, lru_b_a, lru_w_x, lru_b_x, lru_lambda, w_branch, w_out, norm_mlp, w_up, w_down, final_norm):
    bsz, seq, d = x.shape
    depth = ada_w.shape[0]
    gh = gdn_a_log.shape[1]
    dv = gdn_norm.shape[1]
    gqk = (gdn_conv_w.shape[2] - gh * dv) // 2
    dk = gqk // gh
    sh_ = ssd_a_log.shape[1]
    inner = ssd_norm.shape[1]
    hd = inner // sh_
    sbc = (ssd_conv_w.shape[2] - inner) // 2
    ngroups = SSD_GROUPS
    nstate = sbc // ngroups
    lw = lru_lambda.shape[1]
    assert seq % MIXER_BLOCK == 0 and MIXER_BLOCK % SCAN_CHUNK == 0 and 2 * gh + sh_ <= SMALL_WIDTH

    splits = (gqk, gqk, gh * dv, gh * dv, gh, gh, inner, inner, sbc, sbc, sh_, lw, lw, N_BRANCH * d)
    offs = np.concatenate([[0], np.cumsum(splits)])
    cols = lambda w, i: w[:, offs[i]:offs[i + 1]]
    ssd_lane0 = 2 * gh
    widths = (2 * gqk + gh * dv, inner + 2 * sbc, lw, gh * dv + inner + lw, N_BRANCH * d, SMALL_WIDTH)
    dtypes = (BF16, BF16, BF16, BF16, BF16, F32)
    pieces = ((widths[0], True, False, True), (widths[1], True, True, True), (widths[2], True, True, False),
              (widths[3], False, False, False), (widths[4], False, False, False), (widths[5], False, False, False))

    mod = _modulation(c, ada_w, ada_b)
    expand = jnp.asarray(np.kron(np.eye(SMALL_WIDTH, sh_, -ssd_lane0), np.ones((1, hd))), BF16)

    tm = 512
    for l in range(depth):
        sh1, sc1, gt1, sh2, sc2, gt2 = [mod[l, :, None, j * d:(j + 1) * d] for j in range(N_MOD)]
        w = w_in[l]
        small = jnp.concatenate([cols(w, 4), cols(w, 5), cols(w, 10)], axis=1)
        w_all = jnp.concatenate(
            [cols(w, 0), cols(w, 1), cols(w, 2),
             cols(w, 6), cols(w, 8), cols(w, 9),
             cols(w, 11),
             cols(w, 3), cols(w, 7), cols(w, 12),
             cols(w, 13),
             jnp.pad(small, ((0, 0), (0, SMALL_WIDTH - small.shape[1])))], axis=1).astype(BF16)
        conv_w = jnp.concatenate([gdn_conv_w[l], ssd_conv_w[l], lru_conv_w[l]], axis=1)
        conv_b = jnp.concatenate([jnp.zeros((widths[0],), F32), ssd_conv_b[l], lru_conv_b[l]]).reshape(1, -1)
        p_gdn, p_ssd, p_lru, p_z, p_gl, p_s = _inproj(
            x, norm_mix[l].reshape(1, d), sc1, sh1, w_all, conv_w, conv_b, pieces, dtypes, tm)

        y_a = _gdn(p_gdn, p_z, p_s, _lane_row(gdn_a_log[l], gh), _lane_row(gdn_dt_bias[l], gh),
                   gdn_norm[l].reshape(1, dv), gh, dk, dv, MIXER_BLOCK)
        y_b = _ssd(p_ssd, p_z, p_s, _lane_row(ssd_a_log[l], ssd_lane0), _lane_row(ssd_dt_bias[l], ssd_lane0),
                   jnp.repeat(ssd_d[l], hd).reshape(1, inner), ssd_norm[l].reshape(1, inner), expand,
                   sh_, hd, ngroups, nstate, ssd_lane0, MIXER_BLOCK)
        wax = jnp.concatenate([_block_diag(lru_w_a[l]), _block_diag(lru_w_x[l])], axis=1).astype(BF16)
        bax = jnp.concatenate([lru_b_a[l], lru_b_x[l]]).reshape(1, 2 * lw)
        y_c = _lru(p_lru, p_z, wax, bax, lru_lambda[l].reshape(1, lw), MIXER_BLOCK)

        x = _merge_mlp(y_a, y_b, y_c, p_gl, x, gt1, norm_mlp[l].reshape(1, d), sc2, sh2, gt2,
                       w_branch[l].astype(BF16), w_out[l].astype(BF16), w_up[l].astype(BF16),
                       w_down[l].astype(BF16), final_norm.reshape(1, d), l == depth - 1, tm)
    return x
```

```python
import functools

import numpy as np
import jax
import jax.numpy as jnp
from jax import lax
from jax.experimental import pallas as pl
from jax.experimental.pallas import tpu as pltpu

F32 = jnp.float32
BF16 = jnp.bfloat16
HIGHEST = lax.Precision.HIGHEST

RMS_EPS = 1e-6
CONV_WIDTH = 4
LRU_C = 8.0
N_MOD = 6
N_BRANCH = 3

V7X_LANES = 128
V7X_SUBLANES = 8
SCAN_CHUNK = 128
MIXER_BLOCK = 1024
CONV_ROWS = 512
INV_BASE = 16
SSD_GROUPS = 2
SMALL_WIDTH = V7X_LANES
VMEM_LIMIT = 56 * 1024 * 1024

_NT = (((1,), (1,)), ((), ()))
_TN = (((0,), (0,)), ((), ()))


def _dotb(a, b):
    return jnp.dot(a.astype(BF16), b.astype(BF16), preferred_element_type=F32)


def _softplus(x):
    return jnp.maximum(x, 0.0) + jnp.log1p(jnp.exp(-jnp.abs(x)))


def _silu(x):
    return x * jax.nn.sigmoid(x)


def _gelu_tanh(x):
    return 0.5 * x * (1.0 + jnp.tanh(np.sqrt(2.0 / np.pi).astype(np.float32) * (x + 0.044715 * (x * x * x))))


def _rms(x):
    return x * lax.rsqrt(jnp.mean(x * x, axis=-1, keepdims=True) + RMS_EPS)


def _resident(shape):
    nd = len(shape)
    return pl.BlockSpec(shape, lambda *_: (0,) * nd, pipeline_mode=pl.Buffered(1))


def _params(*sem, flags=None):
    return pltpu.CompilerParams(dimension_semantics=sem, vmem_limit_bytes=VMEM_LIMIT, flags=flags)


def _mod_kernel(c_ref, w_ref, b_ref, o_ref):
    c = c_ref[...]
    o_ref[...] = jnp.dot(_silu(c), w_ref[...], precision=HIGHEST, preferred_element_type=F32) + b_ref[...]


def _modulation(c, ada_w, ada_b):
    depth, d, n = ada_w.shape
    bsz = c.shape[0]
    rows = -(-bsz // V7X_SUBLANES) * V7X_SUBLANES
    c_pad = jnp.pad(c, ((0, rows - bsz), (0, 0)))
    tn = n // 4
    out = pl.pallas_call(
        _mod_kernel,
        grid=(depth, n // tn),
        in_specs=[pl.BlockSpec((rows, d), lambda l, j: (0, 0)),
                  pl.BlockSpec((None, d, tn), lambda l, j: (l, 0, j)),
                  pl.BlockSpec((None, 1, tn), lambda l, j: (l, 0, j))],
        out_specs=pl.BlockSpec((None, rows, tn), lambda l, j: (l, 0, j)),
        out_shape=jax.ShapeDtypeStruct((depth, rows, n), F32),
        compiler_params=_params("parallel", "parallel"),
        name="adaln_mod",
    )(c_pad, ada_w, ada_b.reshape(depth, 1, n))
    return out[:, :bsz]


def _inproj_kernel(x_ref, nw_ref, sc_ref, sh_ref, w_ref, cw_ref, cb_ref, *refs, pieces, nt):
    o_refs, carry_ref = refs[:len(pieces)], refs[len(pieces)]
    tm = x_ref.shape[0]
    row8 = lax.broadcasted_iota(jnp.int32, (V7X_SUBLANES, nt), 0)

    @pl.when(pl.program_id(1) == 0)
    def _():
        carry_ref[...] = jnp.zeros(carry_ref.shape, F32)

    h = (_rms(x_ref[...]) * nw_ref[...] * (1.0 + sc_ref[...]) + sh_ref[...]).astype(BF16)

    def column_chunk(o_ref, c0, cw, off, coff, conv, bias, act):
        acc = jnp.dot(h, w_ref[:, off + c0:off + c0 + cw], preferred_element_type=F32)
        yield
        if conv:
            j = (coff + c0) // nt
            cs = slice(coff + c0, coff + c0 + cw)
            carried = carry_ref[j]
            carry_ref[j] = acc[tm - V7X_SUBLANES:, :]
            for r0 in range(0, tm, CONV_ROWS):
                pre = acc[r0:r0 + CONV_ROWS, :]
                prev = carried if r0 == 0 else acc[r0 - V7X_SUBLANES:r0, :]
                out = cw_ref[CONV_WIDTH - 1:CONV_WIDTH, cs] * pre
                for shift in range(1, CONV_WIDTH):
                    head = jnp.where(row8 < shift, pltpu.roll(prev, shift, 0),
                                     pltpu.roll(pre[0:V7X_SUBLANES, :], shift, 0))
                    moved = jnp.concatenate([head, pltpu.roll(pre, shift, 0)[V7X_SUBLANES:, :]], axis=0)
                    out = out + cw_ref[CONV_WIDTH - 1 - shift:CONV_WIDTH - shift, cs] * moved
                if bias:
                    out = out + cb_ref[:, cs]
                if act:
                    out = _silu(out)
                o_ref[r0:r0 + CONV_ROWS, c0:c0 + cw] = out.astype(o_ref.dtype)
        else:
            o_ref[:, c0:c0 + cw] = acc.astype(o_ref.dtype)

    heavy, plain = [], []
    off = coff = 0
    for o_ref, (width, conv, bias, act) in zip(o_refs, pieces):
        for c0 in range(0, width, nt):
            chunk = column_chunk(o_ref, c0, min(nt, width - c0), off, coff, conv, bias, act)
            (heavy if conv else plain).append(chunk)
        off += width
        coff += width if conv else 0
    while heavy or plain:
        for queue in (heavy, plain):
            if queue:
                for _ in queue.pop(0):
                    pass


def _inproj(x, nw, sc, sh, w_all, conv_w, conv_b, pieces, dtypes, tm):
    bsz, seq, d = x.shape
    nt = 512
    conv_width = conv_w.shape[1]
    assert all(p[0] % nt == 0 for p in pieces if p[1])
    tok = lambda width: pl.BlockSpec((None, tm, width), lambda b, i: (b, i, 0))
    per_batch = pl.BlockSpec((None, 1, d), lambda b, i: (b, 0, 0))
    return pl.pallas_call(
        functools.partial(_inproj_kernel, pieces=pieces, nt=nt),
        grid=(bsz, seq // tm),
        in_specs=[tok(d), pl.BlockSpec((1, d), lambda b, i: (0, 0)), per_batch, per_batch,
                  _resident(w_all.shape), pl.BlockSpec((CONV_WIDTH, conv_width), lambda b, i: (0, 0)),
                  pl.BlockSpec((1, conv_width), lambda b, i: (0, 0))],
        out_specs=[tok(p[0]) for p in pieces],
        out_shape=[jax.ShapeDtypeStruct((bsz, seq, p[0]), dt) for p, dt in zip(pieces, dtypes)],
        scratch_shapes=[pltpu.VMEM((conv_width // nt, V7X_SUBLANES, nt), F32)],
        compiler_params=_params("parallel", "arbitrary"),
        name="inproj",
    )(x, nw, sc, sh, w_all, conv_w, conv_b)


def _tri_masks(n):
    row = lax.broadcasted_iota(jnp.int32, (n, n), 0)
    col = lax.broadcasted_iota(jnp.int32, (n, n), 1)
    return row >= col, row > col, row == col


def _masked_exp_diff(mask, col_vec, row_vec):
    return jnp.where(mask, jnp.exp(jnp.where(mask, col_vec - row_vec, 0.0)), 0.0)


def _round_robin(chains):
    active = list(chains)
    while active:
        still = []
        for chain in active:
            try:
                next(chain)
                still.append(chain)
            except StopIteration:
                pass
        active = still


def _dot_exact01(a01, x):
    x1 = x.astype(BF16)
    r1 = x - x1.astype(F32)
    x2 = r1.astype(BF16)
    x3 = (r1 - x2.astype(F32)).astype(BF16)
    return (jnp.dot(a01, x1, preferred_element_type=F32) + jnp.dot(a01, x2, preferred_element_type=F32)
            + jnp.dot(a01, x3, preferred_element_type=F32))


def _dot_exact01_rhs(x, b01):
    x1 = x.astype(BF16)
    r1 = x - x1.astype(F32)
    x2 = r1.astype(BF16)
    x3 = (r1 - x2.astype(F32)).astype(BF16)
    return (jnp.dot(x1, b01, preferred_element_type=F32) + jnp.dot(x2, b01, preferred_element_type=F32)
            + jnp.dot(x3, b01, preferred_element_type=F32))


def _inverse_masks(n):
    row_i = lax.broadcasted_iota(jnp.int32, (n, n), 0)
    col_i = lax.broadcasted_iota(jnp.int32, (n, n), 1)
    masks = [row_i // INV_BASE == col_i // INV_BASE]
    size = INV_BASE
    while size < n:
        masks.append(jnp.logical_and(row_i // (2 * size) == col_i // (2 * size),
                                     jnp.logical_and((row_i // size) % 2 == 1, (col_i // size) % 2 == 0)))
        size *= 2
    return masks


def _gdn_kernel(qkv_ref, z_ref, s_ref, alog_ref, dtb_ref, nw_ref, y_ref, st_ref, *, nheads, dk, dv):
    n = SCAN_CHUNK
    nchunks = qkv_ref.shape[0] // n

    @pl.when(pl.program_id(1) == 0)
    def _():
        st_ref[...] = jnp.zeros(st_ref.shape, F32)

    causal, strict, diag = _tri_masks(n)
    tril_b = causal.astype(BF16)
    eye = diag.astype(F32)
    blk_masks = _inverse_masks(n)
    a_neg = -jnp.exp(alog_ref[...])
    dtb = dtb_ref[...]

    beta_c, gcum_c, gcum_t = [], [], []
    for c in range(nchunks):
        sv = s_ref[c * n:(c + 1) * n, :]
        beta_c.append(jax.nn.sigmoid(sv))
        gcum_c.append(_dot_exact01(tril_b, a_neg * _softplus(sv + dtb)))
        gcum_t.append(gcum_c[-1].T)

    res = {}

    def chain(c, h):
        rows = slice(c * n, (c + 1) * n)
        q = qkv_ref[rows, h * dk:(h + 1) * dk].astype(F32)
        k = qkv_ref[rows, (nheads + h) * dk:(nheads + h + 1) * dk].astype(F32)
        v = qkv_ref[rows, 2 * nheads * dk + h * dv:2 * nheads * dk + (h + 1) * dv].astype(F32)
        q = q * lax.rsqrt(jnp.sum(q * q, axis=-1, keepdims=True) + RMS_EPS) * (dk ** -0.5)
        k = k * lax.rsqrt(jnp.sum(k * k, axis=-1, keepdims=True) + RMS_EPS)
        beta = beta_c[c][:, h:h + 1]
        gc = gcum_c[c][:, nheads + h:nheads + h + 1]
        gr = gcum_t[c][nheads + h:nheads + h + 1, :]
        glast = gcum_c[c][n - 1:n, nheads + h:nheads + h + 1]
        dec = _masked_exp_diff(causal, gc, gr)
        kb = k.astype(BF16)
        kk = lax.dot_general(kb, kb, _NT, preferred_element_type=F32)
        qk = lax.dot_general(q.astype(BF16), kb, _NT, preferred_element_type=F32) * dec
        yield
        m = jnp.where(strict, beta * kk * dec, 0.0)
        nm = -jnp.where(blk_masks[0], m, 0.0)
        inv = eye + nm
        for _ in range(int(np.log2(INV_BASE)) - 1):
            nm_b = nm.astype(BF16)
            nm = jnp.dot(nm_b, nm_b, preferred_element_type=F32)
            yield
            inv = inv + _dotb(inv, nm)
            yield
        for pair_mask in blk_masks[1:]:
            inv_b = inv.astype(BF16)
            low = jnp.dot(jnp.where(pair_mask, m, 0.0).astype(BF16), inv_b, preferred_element_type=F32)
            yield
            inv = inv - jnp.dot(inv_b, low.astype(BF16), preferred_element_type=F32)
            yield
        egc = jnp.exp(gc)
        rhs = jnp.concatenate([beta * v, (beta * egc) * k], axis=1)
        uw = _dotb(inv, rhs).astype(BF16)
        yield
        k_uw = lax.dot_general((k * jnp.exp(glast - gc)).astype(BF16), uw, _TN, preferred_element_type=F32)
        q_uw = jnp.dot(qk.astype(BF16), uw, preferred_element_type=F32)
        yield
        lhs = jnp.concatenate([-k_uw[:, dv:], q * egc - q_uw[:, dv:]], axis=0).astype(BF16)
        res[c, h] = (lhs, k_uw[:, :dv], q_uw[:, :dv], jnp.exp(glast))

    _round_robin([chain(c, h) for c in range(nchunks) for h in range(nheads)])

    states = [st_ref[h] for h in range(nheads)]
    outs = {}
    for c in range(nchunks):
        for h in range(nheads):
            lhs, k_u, q_u, g_tot = res[c, h]
            prod = jnp.dot(lhs, states[h].astype(BF16), preferred_element_type=F32)
            outs[c, h] = prod[dk:] + q_u
            states[h] = states[h] * g_tot + prod[:dk] + k_u
    for h in range(nheads):
        st_ref[h] = states[h]

    for c in range(nchunks):
        rows = slice(c * n, (c + 1) * n)
        for h in range(nheads):
            z = z_ref[rows, h * dv:(h + 1) * dv].astype(F32)
            y_ref[rows, h * dv:(h + 1) * dv] = (_rms(outs[c, h]) * nw_ref[...] * _silu(z)).astype(y_ref.dtype)


def _gdn(p_gdn, p_z, p_s, alog_row, dtb_row, norm_w, nheads, dk, dv, tl):
    bsz, seq, width = p_gdn.shape
    vw = nheads * dv
    row = lambda wd: pl.BlockSpec((1, wd), lambda b, i: (0, 0))
    return pl.pallas_call(
        functools.partial(_gdn_kernel, nheads=nheads, dk=dk, dv=dv),
        grid=(bsz, seq // tl),
        in_specs=[pl.BlockSpec((None, tl, width), lambda b, i: (b, i, 0)),
                  pl.BlockSpec((None, tl, vw), lambda b, i: (b, i, 0)),
                  pl.BlockSpec((None, tl, SMALL_WIDTH), lambda b, i: (b, i, 0)),
                  row(SMALL_WIDTH), row(SMALL_WIDTH), row(dv)],
        out_specs=pl.BlockSpec((None, tl, vw), lambda b, i: (b, i, 0)),
        out_shape=jax.ShapeDtypeStruct((bsz, seq, vw), BF16),
        scratch_shapes=[pltpu.VMEM((nheads, dk, dv), F32)],
        compiler_params=_params("parallel", "arbitrary"),
        name="gdn_mixer",
    )(p_gdn, p_z, p_s, alog_row, dtb_row, norm_w)


def _ssd_kernel(xin_ref, z_ref, s_ref, alog_ref, dtb_ref, dexp_ref, nw_ref, e_ref,
                y_ref, st_ref, yb_ref, *, nheads, hd, ngroups, nstate, lane0):
    n = SCAN_CHUNK
    nchunks = xin_ref.shape[0] // n
    inner = nheads * hd
    hpg = nheads // ngroups
    gw = hpg * hd

    @pl.when(pl.program_id(1) == 0)
    def _():
        st_ref[...] = jnp.zeros(st_ref.shape, F32)

    causal, _, _ = _tri_masks(n)
    tril_b = causal.astype(BF16)
    lane = lax.broadcasted_iota(jnp.int32, (1, SMALL_WIDTH), 1)
    valid = jnp.logical_and(lane >= lane0, lane < lane0 + nheads)
    a_neg = -jnp.exp(alog_ref[...])
    dtb = dtb_ref[...]
    expand = e_ref[...]
    widen = lambda t: jnp.dot(t.astype(BF16), expand, preferred_element_type=F32)

    acum_c, acum_t, ea_c, cd_c, xdt_c, xw_c = [], [], [], [], [], []
    for c in range(nchunks):
        rows = slice(c * n, (c + 1) * n)
        dt_all = jnp.where(valid, _softplus(s_ref[rows, :] + dtb), 0.0)
        acum = _dot_exact01(tril_b, dt_all * a_neg)
        alast = acum[n - 1:n, :]
        acum_c.append(acum)
        acum_t.append(acum.T)
        ea_c.append(widen(jnp.exp(acum)))
        cd_c.append(_dot_exact01_rhs(jnp.broadcast_to(jnp.exp(alast), (V7X_SUBLANES, SMALL_WIDTH)),
                                     expand)[0:1, :])
        xdt = xin_ref[rows, :inner].astype(F32) * widen(dt_all)
        xdt_c.append(xdt.astype(BF16))
        xw_c.append((xdt * widen(jnp.exp(alast - acum))).astype(BF16))

    grown = {}

    def chain(c, g):
        rows = slice(c * n, (c + 1) * n)
        sl = slice(g * gw, (g + 1) * gw)
        bg = xin_ref[rows, inner + g * nstate:inner + (g + 1) * nstate]
        cg = xin_ref[rows, inner + (ngroups + g) * nstate:inner + (ngroups + g + 1) * nstate]
        cb = lax.dot_general(cg, bg, _NT, preferred_element_type=F32)
        grown[c, g] = lax.dot_general(bg, xw_c[c][:, sl], _TN, preferred_element_type=F32)
        yield
        for hh in range(hpg):
            h = g * hpg + hh
            ln = lane0 + h
            lmat = _masked_exp_diff(causal, acum_c[c][:, ln:ln + 1], acum_t[c][ln:ln + 1, :])
            yb_ref[rows, h * hd:(h + 1) * hd] = jnp.dot((cb * lmat).astype(BF16), xdt_c[c][:, h * hd:(h + 1) * hd],
                                                        preferred_element_type=F32)
            yield

    _round_robin([chain(c, g) for c in range(nchunks) for g in range(ngroups)])

    for g in range(ngroups):
        sl = slice(g * gw, (g + 1) * gw)
        state = st_ref[g]
        for c in range(nchunks):
            rows = slice(c * n, (c + 1) * n)
            cg = xin_ref[rows, inner + (ngroups + g) * nstate:inner + (ngroups + g + 1) * nstate]
            y_off = jnp.dot(cg, state.astype(BF16), preferred_element_type=F32) * ea_c[c][:, sl]
            yb_ref[rows, sl] = yb_ref[rows, sl] + y_off
            state = state * cd_c[c][:, sl] + grown[c, g]
        st_ref[g] = state

    for c in range(nchunks):
        rows = slice(c * n, (c + 1) * n)
        y = yb_ref[rows, :] + dexp_ref[...] * xin_ref[rows, :inner].astype(F32)
        gz = y * _silu(z_ref[rows, :].astype(F32))
        for g in range(ngroups):
            sl = slice(g * gw, (g + 1) * gw)
            y_ref[rows, sl] = (_rms(gz[:, sl]) * nw_ref[:, sl]).astype(y_ref.dtype)


def _ssd(p_ssd, p_z, p_s, alog_row, dtb_row, d_exp, norm_w, expand, nheads, hd, ngroups, nstate, lane0, n):
    bsz, seq, width = p_ssd.shape
    inner = nheads * hd
    row = lambda wd: pl.BlockSpec((1, wd), lambda b, i: (0, 0))
    return pl.pallas_call(
        functools.partial(_ssd_kernel, nheads=nheads, hd=hd, ngroups=ngroups, nstate=nstate, lane0=lane0),
        grid=(bsz, seq // n),
        in_specs=[pl.BlockSpec((None, n, width), lambda b, i: (b, i, 0)),
                  pl.BlockSpec((None, n, inner), lambda b, i: (b, i, 1)),
                  pl.BlockSpec((None, n, SMALL_WIDTH), lambda b, i: (b, i, 0)),
                  row(SMALL_WIDTH), row(SMALL_WIDTH), row(inner), row(inner),
                  pl.BlockSpec((SMALL_WIDTH, inner), lambda b, i: (0, 0))],
        out_specs=pl.BlockSpec((None, n, inner), lambda b, i: (b, i, 0)),
        out_shape=jax.ShapeDtypeStruct((bsz, seq, inner), BF16),
        scratch_shapes=[pltpu.VMEM((ngroups, nstate, inner // ngroups), F32),
                        pltpu.VMEM((n, inner), F32)],
        compiler_params=_params("parallel", "arbitrary"),
        name="ssd_mixer",
    )(p_ssd, p_z, p_s, alog_row, dtb_row, d_exp, norm_w, expand)


def _lru_kernel(xin_ref, gate_ref, wax_ref, bax_ref, lam_ref, y_ref, h_ref):
    n, width = xin_ref.shape

    @pl.when(pl.program_id(1) == 0)
    def _():
        h_ref[...] = jnp.zeros(h_ref.shape, F32)

    xc = xin_ref[...].astype(F32)
    rx = jnp.dot(xin_ref[...], wax_ref[...], preferred_element_type=F32) + bax_ref[...]
    r = jax.nn.sigmoid(rx[:, :width])
    ig = jax.nn.sigmoid(rx[:, width:])
    a = jnp.exp(-LRU_C * r * _softplus(-lam_ref[...]))
    y = 1.0 - a * a
    u = jnp.where(y > 0.0, y * lax.rsqrt(y), 0.0) * (ig * xc)

    groups = n // V7X_SUBLANES
    a = a.reshape(groups, V7X_SUBLANES, width)
    u = u.reshape(groups, V7X_SUBLANES, width)
    sub = lax.broadcasted_iota(jnp.int32, (groups, V7X_SUBLANES, width), 1)
    d = 1
    while d < V7X_SUBLANES:
        keep = sub >= d
        u = jnp.where(keep, a * pltpu.roll(u, d, 1) + u, u)
        a = jnp.where(keep, a * pltpu.roll(a, d, 1), a)
        d *= 2
    h = h_ref[...]
    hs = []
    for g in range(groups):
        hs.append(a[g] * h + u[g])
        h = hs[-1][V7X_SUBLANES - 1:V7X_SUBLANES, :]
    h_ref[...] = h
    hs = jnp.concatenate(hs, axis=0)
    y_ref[...] = (hs * _gelu_tanh(gate_ref[...].astype(F32))).astype(y_ref.dtype)


def _lru(p_lru, p_z, wax, bax, lam, tl):
    bsz, seq, width = p_lru.shape
    row = lambda wd: pl.BlockSpec((1, wd), lambda b, i: (0, 0))
    return pl.pallas_call(
        _lru_kernel,
        grid=(bsz, seq // tl),
        in_specs=[pl.BlockSpec((None, tl, width), lambda b, i: (b, i, 0)),
                  pl.BlockSpec((None, tl, width), lambda b, i: (b, i, 2)),
                  pl.BlockSpec((width, 2 * width), lambda b, i: (0, 0)), row(2 * width), row(width)],
        out_specs=pl.BlockSpec((None, tl, width), lambda b, i: (b, i, 0)),
        out_shape=jax.ShapeDtypeStruct((bsz, seq, width), BF16),
        scratch_shapes=[pltpu.VMEM((1, width), F32)],
        compiler_params=_params("parallel", "arbitrary"),
        name="lru_mixer",
    )(p_lru, p_z, wax, bax, lam)


def _merge_mlp_kernel(ya_ref, yb_ref, yc_ref, gl_ref, x_ref, gt1_ref, nw_ref, sc_ref, sh_ref, gt2_ref,
                      wb_ref, wo_ref, wup_ref, wdn_ref, fnw_ref, o_ref, *, final, fc):
    d = x_ref.shape[1]
    merged = None
    for r, y_ref in enumerate((ya_ref, yb_ref, yc_ref)):
        br = jnp.dot(y_ref[...], wb_ref[r], preferred_element_type=F32)
        term = jax.nn.sigmoid(gl_ref[:, r * d:(r + 1) * d].astype(F32)) * br
        merged = term if merged is None else merged + term
    mix = jnp.dot(merged.astype(BF16), wo_ref[...], preferred_element_type=F32)
    x1 = x_ref[...] + gt1_ref[...] * mix

    h = (_rms(x1) * nw_ref[...] * (1.0 + sc_ref[...]) + sh_ref[...]).astype(BF16)
    acc = None
    for c0 in range(0, wup_ref.shape[1], fc):
        act = jnp.maximum(jnp.dot(h, wup_ref[:, c0:c0 + fc], preferred_element_type=F32), 0.0)
        part = jnp.dot((act * act).astype(BF16), wdn_ref[c0:c0 + fc, :], preferred_element_type=F32)
        acc = part if acc is None else acc + part
    x2 = x1 + gt2_ref[...] * acc
    if final:
        x2 = _rms(x2) * fnw_ref[...]
    o_ref[...] = x2


def _merge_mlp(ya, yb, yc, p_gl, x, gt1, nw, sc, sh, gt2, wb, wo, wup, wdn, fnw, final, tm):
    bsz, seq, d = x.shape
    bw = ya.shape[2]
    tok = lambda width: pl.BlockSpec((None, tm, width), lambda b, i: (b, i, 0))
    per_batch = pl.BlockSpec((None, 1, d), lambda b, i: (b, 0, 0))
    row = pl.BlockSpec((1, d), lambda b, i: (0, 0))
    return pl.pallas_call(
        functools.partial(_merge_mlp_kernel, final=final, fc=1024),
        grid=(bsz, seq // tm),
        in_specs=[tok(bw), tok(bw), tok(bw), tok(N_BRANCH * d), tok(d), per_batch, row, per_batch, per_batch,
                  per_batch, _resident(wb.shape), _resident(wo.shape), _resident(wup.shape),
                  _resident(wdn.shape), row],
        out_specs=tok(d),
        out_shape=jax.ShapeDtypeStruct((bsz, seq, d), F32),
        compiler_params=_params("parallel", "parallel"),
        name="merge_mlp",
    )(ya, yb, yc, p_gl, x, gt1, nw, sc, sh, gt2, wb, wo, wup, wdn, fnw)


def _lane_row(vec, lane0):
    return jnp.pad(vec.astype(F32), (lane0, SMALL_WIDTH - lane0 - vec.shape[0])).reshape(1, SMALL_WIDTH)


def _block_diag(w):
    nb, a, b = w.shape
    eye = jnp.eye(nb, dtype=w.dtype)
    return (eye[:, None, :, None] * w[:, :, None, :]).reshape(nb * a, nb * b)


def kernel(x, c, ada_w, ada_b, norm_mix, w_in, gdn_conv_w, gdn_a_log, gdn_dt_bias, gdn_norm, ssd_conv_w, ssd_conv_b, ssd_a_log, ssd_dt_bias, ssd_d, ssd_norm, lru_conv_w, lru_conv_b, lru_w_a, lru_b_a, lru_w_x, lru_b_x, lru_lambda, w_branch, w_out, norm_mlp, w_up, w_down, final_norm):
    bsz, seq, d = x.shape
    depth = ada_w.shape[0]
    gh = gdn_a_log.shape[1]
    dv = gdn_norm.shape[1]
    gqk = (gdn_conv_w.shape[2] - gh * dv) // 2
    dk = gqk // gh
    sh_ = ssd_a_log.shape[1]
    inner = ssd_norm.shape[1]
    hd = inner // sh_
    sbc = (ssd_conv_w.shape[2] - inner) // 2
    ngroups = SSD_GROUPS
    nstate = sbc // ngroups
    lw = lru_lambda.shape[1]
    assert seq % MIXER_BLOCK == 0 and MIXER_BLOCK % SCAN_CHUNK == 0 and 2 * gh + sh_ <= SMALL_WIDTH

    splits = (gqk, gqk, gh * dv, gh * dv, gh, gh, inner, inner, sbc, sbc, sh_, lw, lw, N_BRANCH * d)
    offs = np.concatenate([[0], np.cumsum(splits)])
    cols = lambda w, i: w[:, offs[i]:offs[i + 1]]
    ssd_lane0 = 2 * gh
    widths = (2 * gqk + gh * dv, inner + 2 * sbc, lw, gh * dv + inner + lw, N_BRANCH * d, SMALL_WIDTH)
    dtypes = (BF16, BF16, BF16, BF16, BF16, F32)
    pieces = ((widths[0], True, False, True), (widths[1], True, True, True), (widths[2], True, True, False),
              (widths[3], False, False, False), (widths[4], False, False, False), (widths[5], False, False, False))

    mod = _modulation(c, ada_w, ada_b)
    expand = jnp.asarray(np.kron(np.eye(SMALL_WIDTH, sh_, -ssd_lane0), np.ones((1, hd))), BF16)

    tm = 512
    for l in range(depth):
        sh1, sc1, gt1, sh2, sc2, gt2 = [mod[l, :, None, j * d:(j + 1) * d] for j in range(N_MOD)]
        w = w_in[l]
        small = jnp.concatenate([cols(w, 4), cols(w, 5), cols(w, 10)], axis=1)
        w_all = jnp.concatenate(
            [cols(w, 0), cols(w, 1), cols(w, 2),
             cols(w, 6), cols(w, 8), cols(w, 9),
             cols(w, 11),
             cols(w, 3), cols(w, 7), cols(w, 12),
             cols(w, 13),
             jnp.pad(small, ((0, 0), (0, SMALL_WIDTH - small.shape[1])))], axis=1).astype(BF16)
        conv_w = jnp.concatenate([gdn_conv_w[l], ssd_conv_w[l], lru_conv_w[l]], axis=1)
        conv_b = jnp.concatenate([jnp.zeros((widths[0],), F32), ssd_conv_b[l], lru_conv_b[l]]).reshape(1, -1)
        p_gdn, p_ssd, p_lru, p_z, p_gl, p_s = _inproj(
            x, norm_mix[l].reshape(1, d), sc1, sh1, w_all, conv_w, conv_b, pieces, dtypes, tm)

        y_a = _gdn(p_gdn, p_z, p_s, _lane_row(gdn_a_log[l], gh), _lane_row(gdn_dt_bias[l], gh),
                   gdn_norm[l].reshape(1, dv), gh, dk, dv, MIXER_BLOCK)
        y_b = _ssd(p_ssd, p_z, p_s, _lane_row(ssd_a_log[l], ssd_lane0), _lane_row(ssd_dt_bias[l], ssd_lane0),
                   jnp.repeat(ssd_d[l], hd).reshape(1, inner), ssd_norm[l].reshape(1, inner), expand,
                   sh_, hd, ngroups, nstate, ssd_lane0, MIXER_BLOCK)
        wax = jnp.concatenate([_block_diag(lru_w_a[l]), _block_diag(lru_w_x[l])], axis=1).astype(BF16)
        bax = jnp.concatenate([lru_b_a[l], lru_b_x[l]]).reshape(1, 2 * lw)
        y_c = _lru(p_lru, p_z, wax, bax, lru_lambda[l].reshape(1, lw), MIXER_BLOCK)

        x = _merge_mlp(y_a, y_b, y_c, p_gl, x, gt1, norm_mlp[l].reshape(1, d), sc2, sh2, gt2,
                       w_branch[l].astype(BF16), w_out[l].astype(BF16), w_up[l].astype(BF16),
                       w_down[l].astype(BF16), final_norm.reshape(1, d), l == depth - 1, tm)
    return x
```

```python
import functools

import numpy as np
import jax
import jax.numpy as jnp
from jax import lax
from jax.experimental import pallas as pl
from jax.experimental.pallas import tpu as pltpu

F32 = jnp.float32
BF16 = jnp.bfloat16
HIGHEST = lax.Precision.HIGHEST

RMS_EPS = 1e-6
CONV_WIDTH = 4
LRU_C = 8.0
N_MOD = 6
N_BRANCH = 3

V7X_LANES = 128
V7X_SUBLANES = 8
SCAN_CHUNK = 128
MIXER_BLOCK = 1024
CONV_ROWS = 512
INV_BASE = 16
SSD_GROUPS = 2
SMALL_WIDTH = V7X_LANES
VMEM_LIMIT = 56 * 1024 * 1024

_NT = (((1,), (1,)), ((), ()))
_TN = (((0,), (0,)), ((), ()))


def _dotb(a, b):
    return jnp.dot(a.astype(BF16), b.astype(BF16), preferred_element_type=F32)


def _softplus(x):
    return jnp.maximum(x, 0.0) + jnp.log1p(jnp.exp(-jnp.abs(x)))


def _silu(x):
    half = 0.5 * x
    return half + half * jnp.tanh(half)


def _gelu_tanh(x):
    return 0.5 * x * (1.0 + jnp.tanh(np.sqrt(2.0 / np.pi).astype(np.float32) * (x + 0.044715 * (x * x * x))))


def _rms(x):
    return x * lax.rsqrt(jnp.mean(x * x, axis=-1, keepdims=True) + RMS_EPS)


def _resident(shape):
    nd = len(shape)
    return pl.BlockSpec(shape, lambda *_: (0,) * nd, pipeline_mode=pl.Buffered(1))


def _params(*sem):
    return pltpu.CompilerParams(dimension_semantics=sem, vmem_limit_bytes=VMEM_LIMIT)


def _mod_kernel(c_ref, w_ref, b_ref, o_ref):
    c = c_ref[...]
    o_ref[...] = jnp.dot(_silu(c), w_ref[...], precision=HIGHEST, preferred_element_type=F32) + b_ref[...]


def _modulation(c, ada_w, ada_b):
    depth, d, n = ada_w.shape
    bsz = c.shape[0]
    rows = -(-bsz // V7X_SUBLANES) * V7X_SUBLANES
    c_pad = jnp.pad(c, ((0, rows - bsz), (0, 0)))
    tn = n // 4
    out = pl.pallas_call(
        _mod_kernel,
        grid=(depth, n // tn),
        in_specs=[pl.BlockSpec((rows, d), lambda l, j: (0, 0)),
                  pl.BlockSpec((None, d, tn), lambda l, j: (l, 0, j)),
                  pl.BlockSpec((None, 1, tn), lambda l, j: (l, 0, j))],
        out_specs=pl.BlockSpec((None, rows, tn), lambda l, j: (l, 0, j)),
        out_shape=jax.ShapeDtypeStruct((depth, rows, n), F32),
        compiler_params=_params("parallel", "parallel"),
        name="adaln_mod",
    )(c_pad, ada_w, ada_b.reshape(depth, 1, n))
    return out[:, :bsz]


def _inproj_kernel(x_ref, nw_ref, sc_ref, sh_ref, w_ref, cw_ref, cb_ref, *refs, pieces, nt):
    o_refs, carry_ref = refs[:len(pieces)], refs[len(pieces)]
    tm = x_ref.shape[0]
    row8 = lax.broadcasted_iota(jnp.int32, (V7X_SUBLANES, nt), 0)

    @pl.when(pl.program_id(1) == 0)
    def _():
        carry_ref[...] = jnp.zeros(carry_ref.shape, F32)

    h = (_rms(x_ref[...]) * nw_ref[...] * (1.0 + sc_ref[...]) + sh_ref[...]).astype(BF16)

    def column_chunk(o_ref, c0, cw, off, coff, conv, bias, act):
        acc = jnp.dot(h, w_ref[:, off + c0:off + c0 + cw], preferred_element_type=F32)
        yield
        if conv:
            j = (coff + c0) // nt
            cs = slice(coff + c0, coff + c0 + cw)
            carried = carry_ref[j]
            carry_ref[j] = acc[tm - V7X_SUBLANES:, :]
            scale = 0.5 if act else 1.0
            taps = [cw_ref[k:k + 1, cs] * scale for k in range(CONV_WIDTH)]
            for r0 in range(0, tm, CONV_ROWS):
                pre = acc[r0:r0 + CONV_ROWS, :]
                prev = carried if r0 == 0 else acc[r0 - V7X_SUBLANES:r0, :]
                out = taps[CONV_WIDTH - 1] * pre
                for shift in range(1, CONV_WIDTH):
                    head = jnp.where(row8 < shift, pltpu.roll(prev, shift, 0),
                                     pltpu.roll(pre[0:V7X_SUBLANES, :], shift, 0))
                    moved = jnp.concatenate([head, pltpu.roll(pre, shift, 0)[V7X_SUBLANES:, :]], axis=0)
                    out = out + taps[CONV_WIDTH - 1 - shift] * moved
                if bias:
                    out = out + cb_ref[:, cs] * scale
                if act:
                    out = out + out * jnp.tanh(out)
                o_ref[r0:r0 + CONV_ROWS, c0:c0 + cw] = out.astype(o_ref.dtype)
        else:
            o_ref[:, c0:c0 + cw] = acc.astype(o_ref.dtype)

    heavy, plain = [], []
    off = coff = 0
    for o_ref, (width, conv, bias, act) in zip(o_refs, pieces):
        for c0 in range(0, width, nt):
            chunk = column_chunk(o_ref, c0, min(nt, width - c0), off, coff, conv, bias, act)
            (heavy if conv else plain).append(chunk)
        off += width
        coff += width if conv else 0
    while heavy or plain:
        for queue in (heavy, plain):
            if queue:
                for _ in queue.pop(0):
                    pass


def _inproj(x, nw, sc, sh, w_all, conv_w, conv_b, pieces, dtypes, tm):
    bsz, seq, d = x.shape
    nt = 512
    conv_width = conv_w.shape[1]
    assert all(p[0] % nt == 0 for p in pieces if p[1])
    tok = lambda width: pl.BlockSpec((None, tm, width), lambda b, i: (b, i, 0))
    per_batch = pl.BlockSpec((None, 1, d), lambda b, i: (b, 0, 0))
    return pl.pallas_call(
        functools.partial(_inproj_kernel, pieces=pieces, nt=nt),
        grid=(bsz, seq // tm),
        in_specs=[tok(d), pl.BlockSpec((1, d), lambda b, i: (0, 0)), per_batch, per_batch,
                  _resident(w_all.shape), pl.BlockSpec((CONV_WIDTH, conv_width), lambda b, i: (0, 0)),
                  pl.BlockSpec((1, conv_width), lambda b, i: (0, 0))],
        out_specs=[tok(p[0]) for p in pieces],
        out_shape=[jax.ShapeDtypeStruct((bsz, seq, p[0]), dt) for p, dt in zip(pieces, dtypes)],
        scratch_shapes=[pltpu.VMEM((conv_width // nt, V7X_SUBLANES, nt), F32)],
        compiler_params=_params("parallel", "arbitrary"),
        name="inproj",
    )(x, nw, sc, sh, w_all, conv_w, conv_b)


def _tri_masks(n):
    row = lax.broadcasted_iota(jnp.int32, (n, n), 0)
    col = lax.broadcasted_iota(jnp.int32, (n, n), 1)
    return row >= col, row > col, row == col


def _masked_exp_diff(mask, col_vec, row_vec):
    return jnp.where(mask, jnp.exp(jnp.where(mask, col_vec - row_vec, 0.0)), 0.0)


def _round_robin(chains):
    active = list(chains)
    while active:
        still = []
        for chain in active:
            try:
                next(chain)
                still.append(chain)
            except StopIteration:
                pass
        active = still


def _dot_exact01(a01, x):
    x1 = x.astype(BF16)
    r1 = x - x1.astype(F32)
    x2 = r1.astype(BF16)
    x3 = (r1 - x2.astype(F32)).astype(BF16)
    return (jnp.dot(a01, x1, preferred_element_type=F32) + jnp.dot(a01, x2, preferred_element_type=F32)
            + jnp.dot(a01, x3, preferred_element_type=F32))


def _dot_exact01_rhs(x, b01):
    x1 = x.astype(BF16)
    r1 = x - x1.astype(F32)
    x2 = r1.astype(BF16)
    x3 = (r1 - x2.astype(F32)).astype(BF16)
    return (jnp.dot(x1, b01, preferred_element_type=F32) + jnp.dot(x2, b01, preferred_element_type=F32)
            + jnp.dot(x3, b01, preferred_element_type=F32))


def _inverse_masks(n):
    row_i = lax.broadcasted_iota(jnp.int32, (n, n), 0)
    col_i = lax.broadcasted_iota(jnp.int32, (n, n), 1)
    masks = [row_i // INV_BASE == col_i // INV_BASE]
    size = INV_BASE
    while size < n:
        masks.append(jnp.logical_and(row_i // (2 * size) == col_i // (2 * size),
                                     jnp.logical_and((row_i // size) % 2 == 1, (col_i // size) % 2 == 0)))
        size *= 2
    return masks


def _gdn_kernel(qkv_ref, z_ref, s_ref, alog_ref, dtb_ref, nw_ref, y_ref, st_ref, *, nheads, dk, dv):
    n = SCAN_CHUNK
    nchunks = qkv_ref.shape[0] // n

    @pl.when(pl.program_id(1) == 0)
    def _():
        st_ref[...] = jnp.zeros(st_ref.shape, F32)

    causal, strict, diag = _tri_masks(n)
    tril_b = causal.astype(BF16)
    eye = diag.astype(F32)
    blk_masks = _inverse_masks(n)
    a_neg = -jnp.exp(alog_ref[...])
    dtb = dtb_ref[...]

    beta_c, gcum_c, gcum_t = [], [], []
    for c in range(nchunks):
        sv = s_ref[c * n:(c + 1) * n, :]
        beta_c.append(jax.nn.sigmoid(sv))
        gcum_c.append(_dot_exact01(tril_b, a_neg * _softplus(sv + dtb)))
        gcum_t.append(gcum_c[-1].T)

    res = {}

    def chain(c, h):
        rows = slice(c * n, (c + 1) * n)
        q = qkv_ref[rows, h * dk:(h + 1) * dk].astype(F32)
        k = qkv_ref[rows, (nheads + h) * dk:(nheads + h + 1) * dk].astype(F32)
        v = qkv_ref[rows, 2 * nheads * dk + h * dv:2 * nheads * dk + (h + 1) * dv].astype(F32)
        q = q * lax.rsqrt(jnp.sum(q * q, axis=-1, keepdims=True) + RMS_EPS) * (dk ** -0.5)
        k = k * lax.rsqrt(jnp.sum(k * k, axis=-1, keepdims=True) + RMS_EPS)
        beta = beta_c[c][:, h:h + 1]
        gc = gcum_c[c][:, nheads + h:nheads + h + 1]
        gr = gcum_t[c][nheads + h:nheads + h + 1, :]
        glast = gcum_c[c][n - 1:n, nheads + h:nheads + h + 1]
        dec = _masked_exp_diff(causal, gc, gr)
        kb = k.astype(BF16)
        kk = lax.dot_general(kb, kb, _NT, preferred_element_type=F32)
        qk = lax.dot_general(q.astype(BF16), kb, _NT, preferred_element_type=F32) * dec
        yield
        m = jnp.where(strict, beta * kk * dec, 0.0)
        nm = -jnp.where(blk_masks[0], m, 0.0)
        inv = eye + nm
        for _ in range(int(np.log2(INV_BASE)) - 1):
            nm_b = nm.astype(BF16)
            nm = jnp.dot(nm_b, nm_b, preferred_element_type=F32)
            yield
            inv = inv + _dotb(inv, nm)
            yield
        for pair_mask in blk_masks[1:]:
            inv_b = inv.astype(BF16)
            low = jnp.dot(jnp.where(pair_mask, m, 0.0).astype(BF16), inv_b, preferred_element_type=F32)
            yield
            inv = inv - jnp.dot(inv_b, low.astype(BF16), preferred_element_type=F32)
            yield
        egc = jnp.exp(gc)
        rhs = jnp.concatenate([beta * v, (beta * egc) * k], axis=1)
        uw = _dotb(inv, rhs).astype(BF16)
        yield
        k_uw = lax.dot_general((k * jnp.exp(glast - gc)).astype(BF16), uw, _TN, preferred_element_type=F32)
        q_uw = jnp.dot(qk.astype(BF16), uw, preferred_element_type=F32)
        yield
        lhs = jnp.concatenate([-k_uw[:, dv:], q * egc - q_uw[:, dv:]], axis=0).astype(BF16)
        res[c, h] = (lhs, k_uw[:, :dv], q_uw[:, :dv], jnp.exp(glast))

    _round_robin([chain(c, h) for c in range(nchunks) for h in range(nheads)])

    states = [st_ref[h] for h in range(nheads)]
    outs = {}
    for c in range(nchunks):
        for h in range(nheads):
            lhs, k_u, q_u, g_tot = res[c, h]
            prod = jnp.dot(lhs, states[h].astype(BF16), preferred_element_type=F32)
            outs[c, h] = prod[dk:] + q_u
            states[h] = states[h] * g_tot + prod[:dk] + k_u
    for h in range(nheads):
        st_ref[h] = states[h]

    for c in range(nchunks):
        rows = slice(c * n, (c + 1) * n)
        for h in range(nheads):
            z = z_ref[rows, h * dv:(h + 1) * dv].astype(F32)
            y_ref[rows, h * dv:(h + 1) * dv] = (_rms(outs[c, h]) * nw_ref[...] * _silu(z)).astype(y_ref.dtype)


def _gdn(p_gdn, p_z, p_s, alog_row, dtb_row, norm_w, nheads, dk, dv, tl):
    bsz, seq, width = p_gdn.shape
    vw = nheads * dv
    row = lambda wd: pl.BlockSpec((1, wd), lambda b, i: (0, 0))
    return pl.pallas_call(
        functools.partial(_gdn_kernel, nheads=nheads, dk=dk, dv=dv),
        grid=(bsz, seq // tl),
        in_specs=[pl.BlockSpec((None, tl, width), lambda b, i: (b, i, 0)),
                  pl.BlockSpec((None, tl, vw), lambda b, i: (b, i, 0)),
                  pl.BlockSpec((None, tl, SMALL_WIDTH), lambda b, i: (b, i, 0)),
                  row(SMALL_WIDTH), row(SMALL_WIDTH), row(dv)],
        out_specs=pl.BlockSpec((None, tl, vw), lambda b, i: (b, i, 0)),
        out_shape=jax.ShapeDtypeStruct((bsz, seq, vw), BF16),
        scratch_shapes=[pltpu.VMEM((nheads, dk, dv), F32)],
        compiler_params=_params("parallel", "arbitrary"),
        name="gdn_mixer",
    )(p_gdn, p_z, p_s, alog_row, dtb_row, norm_w)


def _ssd_kernel(xin_ref, z_ref, s_ref, alog_ref, dtb_ref, dexp_ref, nw_ref, e_ref,
                y_ref, st_ref, yb_ref, *, nheads, hd, ngroups, nstate, lane0):
    n = SCAN_CHUNK
    nchunks = xin_ref.shape[0] // n
    inner = nheads * hd
    hpg = nheads // ngroups
    gw = hpg * hd

    @pl.when(pl.program_id(1) == 0)
    def _():
        st_ref[...] = jnp.zeros(st_ref.shape, F32)

    causal, _, _ = _tri_masks(n)
    tril_b = causal.astype(BF16)
    lane = lax.broadcasted_iota(jnp.int32, (1, SMALL_WIDTH), 1)
    valid = jnp.logical_and(lane >= lane0, lane < lane0 + nheads)
    a_neg = -jnp.exp(alog_ref[...])
    dtb = dtb_ref[...]
    expand = e_ref[...]
    widen = lambda t: jnp.dot(t.astype(BF16), expand, preferred_element_type=F32)

    acum_c, acum_t, ea_c, cd_c, xdt_c, xw_c = [], [], [], [], [], []
    for c in range(nchunks):
        rows = slice(c * n, (c + 1) * n)
        dt_all = jnp.where(valid, _softplus(s_ref[rows, :] + dtb), 0.0)
        acum = _dot_exact01(tril_b, dt_all * a_neg)
        alast = acum[n - 1:n, :]
        acum_c.append(acum)
        acum_t.append(acum.T)
        ea_c.append(widen(jnp.exp(acum)))
        cd_c.append(_dot_exact01_rhs(jnp.broadcast_to(jnp.exp(alast), (V7X_SUBLANES, SMALL_WIDTH)),
                                     expand)[0:1, :])
        xdt = xin_ref[rows, :inner].astype(F32) * widen(dt_all)
        xdt_c.append(xdt.astype(BF16))
        xw_c.append((xdt * widen(jnp.exp(alast - acum))).astype(BF16))

    grown = {}

    def chain(c, g):
        rows = slice(c * n, (c + 1) * n)
        sl = slice(g * gw, (g + 1) * gw)
        bg = xin_ref[rows, inner + g * nstate:inner + (g + 1) * nstate]
        cg = xin_ref[rows, inner + (ngroups + g) * nstate:inner + (ngroups + g + 1) * nstate]
        cb = lax.dot_general(cg, bg, _NT, preferred_element_type=F32)
        grown[c, g] = lax.dot_general(bg, xw_c[c][:, sl], _TN, preferred_element_type=F32)
        yield
        for hh in range(hpg):
            h = g * hpg + hh
            ln = lane0 + h
            lmat = _masked_exp_diff(causal, acum_c[c][:, ln:ln + 1], acum_t[c][ln:ln + 1, :])
            yb_ref[rows, h * hd:(h + 1) * hd] = jnp.dot((cb * lmat).astype(BF16), xdt_c[c][:, h * hd:(h + 1) * hd],
                                                        preferred_element_type=F32)
            yield

    _round_robin([chain(c, g) for c in range(nchunks) for g in range(ngroups)])

    for g in range(ngroups):
        sl = slice(g * gw, (g + 1) * gw)
        state = st_ref[g]
        for c in range(nchunks):
            rows = slice(c * n, (c + 1) * n)
            cg = xin_ref[rows, inner + (ngroups + g) * nstate:inner + (ngroups + g + 1) * nstate]
            y_off = jnp.dot(cg, state.astype(BF16), preferred_element_type=F32) * ea_c[c][:, sl]
            yb_ref[rows, sl] = yb_ref[rows, sl] + y_off
            state = state * cd_c[c][:, sl] + grown[c, g]
        st_ref[g] = state

    for c in range(nchunks):
        rows = slice(c * n, (c + 1) * n)
        y = yb_ref[rows, :] + dexp_ref[...] * xin_ref[rows, :inner].astype(F32)
        gz = y * _silu(z_ref[rows, :].astype(F32))
        for g in range(ngroups):
            sl = slice(g * gw, (g + 1) * gw)
            y_ref[rows, sl] = (_rms(gz[:, sl]) * nw_ref[:, sl]).astype(y_ref.dtype)


def _ssd(p_ssd, p_z, p_s, alog_row, dtb_row, d_exp, norm_w, expand, nheads, hd, ngroups, nstate, lane0, n):
    bsz, seq, width = p_ssd.shape
    inner = nheads * hd
    row = lambda wd: pl.BlockSpec((1, wd), lambda b, i: (0, 0))
    return pl.pallas_call(
        functools.partial(_ssd_kernel, nheads=nheads, hd=hd, ngroups=ngroups, nstate=nstate, lane0=lane0),
        grid=(bsz, seq // n),
        in_specs=[pl.BlockSpec((None, n, width), lambda b, i: (b, i, 0)),
                  pl.BlockSpec((None, n, inner), lambda b, i: (b, i, 1)),
                  pl.BlockSpec((None, n, SMALL_WIDTH), lambda b, i: (b, i, 0)),
                  row(SMALL_WIDTH), row(SMALL_WIDTH), row(inner), row(inner),
                  pl.BlockSpec((SMALL_WIDTH, inner), lambda b, i: (0, 0))],
        out_specs=pl.BlockSpec((None, n, inner), lambda b, i: (b, i, 0)),
        out_shape=jax.ShapeDtypeStruct((bsz, seq, inner), BF16),
        scratch_shapes=[pltpu.VMEM((ngroups, nstate, inner // ngroups), F32),
                        pltpu.VMEM((n, inner), F32)],
        compiler_params=_params("parallel", "arbitrary"),
        name="ssd_mixer",
    )(p_ssd, p_z, p_s, alog_row, dtb_row, d_exp, norm_w, expand)


def _lru_kernel(xin_ref, gate_ref, wax_ref, bax_ref, lam_ref, y_ref, h_ref):
    n, width = xin_ref.shape

    @pl.when(pl.program_id(1) == 0)
    def _():
        h_ref[...] = jnp.zeros(h_ref.shape, F32)

    xc = xin_ref[...].astype(F32)
    rx = jnp.dot(xin_ref[...], wax_ref[...], preferred_element_type=F32) + bax_ref[...]
    r = jax.nn.sigmoid(rx[:, :width])
    ig = jax.nn.sigmoid(rx[:, width:])
    a = jnp.exp(-LRU_C * r * _softplus(-lam_ref[...]))
    y = 1.0 - a * a
    u = jnp.where(y > 0.0, y * lax.rsqrt(y), 0.0) * (ig * xc)

    groups = n // V7X_SUBLANES
    a = a.reshape(groups, V7X_SUBLANES, width)
    u = u.reshape(groups, V7X_SUBLANES, width)
    sub = lax.broadcasted_iota(jnp.int32, (groups, V7X_SUBLANES, width), 1)
    d = 1
    while d < V7X_SUBLANES:
        keep = sub >= d
        u = jnp.where(keep, a * pltpu.roll(u, d, 1) + u, u)
        a = jnp.where(keep, a * pltpu.roll(a, d, 1), a)
        d *= 2
    h = h_ref[...]
    hs = []
    for g in range(groups):
        hs.append(a[g] * h + u[g])
        h = hs[-1][V7X_SUBLANES - 1:V7X_SUBLANES, :]
    h_ref[...] = h
    hs = jnp.concatenate(hs, axis=0)
    y_ref[...] = (hs * _gelu_tanh(gate_ref[...].astype(F32))).astype(y_ref.dtype)


def _lru(p_lru, p_z, wax, bax, lam, tl):
    bsz, seq, width = p_lru.shape
    row = lambda wd: pl.BlockSpec((1, wd), lambda b, i: (0, 0))
    return pl.pallas_call(
        _lru_kernel,
        grid=(bsz, seq // tl),
        in_specs=[pl.BlockSpec((None, tl, width), lambda b, i: (b, i, 0)),
                  pl.BlockSpec((None, tl, width), lambda b, i: (b, i, 2)),
                  pl.BlockSpec((width, 2 * width), lambda b, i: (0, 0)), row(2 * width), row(width)],
        out_specs=pl.BlockSpec((None, tl, width), lambda b, i: (b, i, 0)),
        out_shape=jax.ShapeDtypeStruct((bsz, seq, width), BF16),
        scratch_shapes=[pltpu.VMEM((1, width), F32)],
        compiler_params=_params("parallel", "arbitrary"),
        name="lru_mixer",
    )(p_lru, p_z, wax, bax, lam)


def _merge_mlp_kernel(ya_ref, yb_ref, yc_ref, gl_ref, x_ref, gt1_ref, nw_ref, sc_ref, sh_ref, gt2_ref,
                      wb_ref, wo_ref, wup_ref, wdn_ref, fnw_ref, o_ref, *, final, fc):
    d = x_ref.shape[1]
    merged = None
    for r, y_ref in enumerate((ya_ref, yb_ref, yc_ref)):
        br = jnp.dot(y_ref[...], wb_ref[r], preferred_element_type=F32)
        term = jax.nn.sigmoid(gl_ref[:, r * d:(r + 1) * d].astype(F32)) * br
        merged = term if merged is None else merged + term
    mix = jnp.dot(merged.astype(BF16), wo_ref[...], preferred_element_type=F32)
    x1 = x_ref[...] + gt1_ref[...] * mix

    h = (_rms(x1) * nw_ref[...] * (1.0 + sc_ref[...]) + sh_ref[...]).astype(BF16)
    acc = None
    for c0 in range(0, wup_ref.shape[1], fc):
        act = jnp.maximum(jnp.dot(h, wup_ref[:, c0:c0 + fc], preferred_element_type=F32), 0.0)
        part = jnp.dot((act * act).astype(BF16), wdn_ref[c0:c0 + fc, :], preferred_element_type=F32)
        acc = part if acc is None else acc + part
    x2 = x1 + gt2_ref[...] * acc
    if final:
        x2 = _rms(x2) * fnw_ref[...]
    o_ref[...] = x2


def _merge_mlp(ya, yb, yc, p_gl, x, gt1, nw, sc, sh, gt2, wb, wo, wup, wdn, fnw, final, tm):
    bsz, seq, d = x.shape
    bw = ya.shape[2]
    tok = lambda width: pl.BlockSpec((None, tm, width), lambda b, i: (b, i, 0))
    per_batch = pl.BlockSpec((None, 1, d), lambda b, i: (b, 0, 0))
    row = pl.BlockSpec((1, d), lambda b, i: (0, 0))
    return pl.pallas_call(
        functools.partial(_merge_mlp_kernel, final=final, fc=1024),
        grid=(bsz, seq // tm),
        in_specs=[tok(bw), tok(bw), tok(bw), tok(N_BRANCH * d), tok(d), per_batch, row, per_batch, per_batch,
                  per_batch, _resident(wb.shape), _resident(wo.shape), _resident(wup.shape),
                  _resident(wdn.shape), row],
        out_specs=tok(d),
        out_shape=jax.ShapeDtypeStruct((bsz, seq, d), F32),
        compiler_params=_params("parallel", "parallel"),
        name="merge_mlp",
    )(ya, yb, yc, p_gl, x, gt1, nw, sc, sh, gt2, wb, wo, wup, wdn, fnw)


def _lane_row(vec, lane0):
    return jnp.pad(vec.astype(F32), (lane0, SMALL_WIDTH - lane0 - vec.shape[0])).reshape(1, SMALL_WIDTH)


def _block_diag(w):
    nb, a, b = w.shape
    eye = jnp.eye(nb, dtype=w.dtype)
    return (eye[:, None, :, None] * w[:, :, None, :]).reshape(nb * a, nb * b)


def kernel(x, c, ada_w, ada_b, norm_mix, w_in, gdn_conv_w, gdn_a_log, gdn_dt_bias, gdn_norm, ssd_conv_w, ssd_conv_b, ssd_a_log, ssd_dt_bias, ssd_d, ssd_norm, lru_conv_w, lru_conv_b, lru_w_a, lru_b_a, lru_w_x, lru_b_x, lru_lambda, w_branch, w_out, norm_mlp, w_up, w_down, final_norm):
    bsz, seq, d = x.shape
    depth = ada_w.shape[0]
    gh = gdn_a_log.shape[1]
    dv = gdn_norm.shape[1]
    gqk = (gdn_conv_w.shape[2] - gh * dv) // 2
    dk = gqk // gh
    sh_ = ssd_a_log.shape[1]
    inner = ssd_norm.shape[1]
    hd = inner // sh_
    sbc = (ssd_conv_w.shape[2] - inner) // 2
    ngroups = SSD_GROUPS
    nstate = sbc // ngroups
    lw = lru_lambda.shape[1]
    assert seq % MIXER_BLOCK == 0 and MIXER_BLOCK % SCAN_CHUNK == 0 and 2 * gh + sh_ <= SMALL_WIDTH

    splits = (gqk, gqk, gh * dv, gh * dv, gh, gh, inner, inner, sbc, sbc, sh_, lw, lw, N_BRANCH * d)
    offs = np.concatenate([[0], np.cumsum(splits)])
    cols = lambda w, i: w[:, offs[i]:offs[i + 1]]
    ssd_lane0 = 2 * gh
    widths = (2 * gqk + gh * dv, inner + 2 * sbc, lw, gh * dv + inner + lw, N_BRANCH * d, SMALL_WIDTH)
    dtypes = (BF16, BF16, BF16, BF16, BF16, F32)
    pieces = ((widths[0], True, False, True), (widths[1], True, True, True), (widths[2], True, True, False),
              (widths[3], False, False, False), (widths[4], False, False, False), (widths[5], False, False, False))

    mod = _modulation(c, ada_w, ada_b)
    expand = jnp.asarray(np.kron(np.eye(SMALL_WIDTH, sh_, -ssd_lane0), np.ones((1, hd))), BF16)

    tm = 512
    for l in range(depth):
        sh1, sc1, gt1, sh2, sc2, gt2 = [mod[l, :, None, j * d:(j + 1) * d] for j in range(N_MOD)]
        w = w_in[l]
        small = jnp.concatenate([cols(w, 4), cols(w, 5), cols(w, 10)], axis=1)
        w_all = jnp.concatenate(
            [cols(w, 0), cols(w, 1), cols(w, 2),
             cols(w, 6), cols(w, 8), cols(w, 9),
             cols(w, 11),
             cols(w, 3), cols(w, 7), cols(w, 12),
             cols(w, 13),
             jnp.pad(small, ((0, 0), (0, SMALL_WIDTH - small.shape[1])))], axis=1).astype(BF16)
        conv_w = jnp.concatenate([gdn_conv_w[l], ssd_conv_w[l], lru_conv_w[l]], axis=1)
        conv_b = jnp.concatenate([jnp.zeros((widths[0],), F32), ssd_conv_b[l], lru_conv_b[l]]).reshape(1, -1)
        p_gdn, p_ssd, p_lru, p_z, p_gl, p_s = _inproj(
            x, norm_mix[l].reshape(1, d), sc1, sh1, w_all, conv_w, conv_b, pieces, dtypes, tm)

        y_a = _gdn(p_gdn, p_z, p_s, _lane_row(gdn_a_log[l], gh), _lane_row(gdn_dt_bias[l], gh),
                   gdn_norm[l].reshape(1, dv), gh, dk, dv, MIXER_BLOCK)
        y_b = _ssd(p_ssd, p_z, p_s, _lane_row(ssd_a_log[l], ssd_lane0), _lane_row(ssd_dt_bias[l], ssd_lane0),
                   jnp.repeat(ssd_d[l], hd).reshape(1, inner), ssd_norm[l].reshape(1, inner), expand,
                   sh_, hd, ngroups, nstate, ssd_lane0, MIXER_BLOCK)
        wax = jnp.concatenate([_block_diag(lru_w_a[l]), _block_diag(lru_w_x[l])], axis=1).astype(BF16)
        bax = jnp.concatenate([lru_b_a[l], lru_b_x[l]]).reshape(1, 2 * lw)
        y_c = _lru(p_lru, p_z, wax, bax, lru_lambda[l].reshape(1, lw), MIXER_BLOCK)

        x = _merge_mlp(y_a, y_b, y_c, p_gl, x, gt1, norm_mlp[l].reshape(1, d), sc2, sh2, gt2,
                       w_branch[l].astype(BF16), w_out[l].astype(BF16), w_up[l].astype(BF16),
                       w_down[l].astype(BF16), final_norm.reshape(1, d), l == depth - 1, tm)
    return x
```

```python
import functools

import numpy as np
import jax
import jax.numpy as jnp
from jax import lax
from jax.experimental import pallas as pl
from jax.experimental.pallas import tpu as pltpu

F32 = jnp.float32
BF16 = jnp.bfloat16
HIGHEST = lax.Precision.HIGHEST

RMS_EPS = 1e-6
CONV_WIDTH = 4
LRU_C = 8.0
N_MOD = 6
N_BRANCH = 3

V7X_LANES = 128
V7X_SUBLANES = 8
SCAN_CHUNK = 128
MIXER_BLOCK = 1024
SCAN_BLOCK = 2048
CONV_ROWS = 512
INV_BASE = 16
SSD_GROUPS = 2
SMALL_WIDTH = V7X_LANES
VMEM_LIMIT = 56 * 1024 * 1024

_NT = (((1,), (1,)), ((), ()))
_TN = (((0,), (0,)), ((), ()))


def _dotb(a, b):
    return jnp.dot(a.astype(BF16), b.astype(BF16), preferred_element_type=F32)


def _softplus(x):
    return jnp.maximum(x, 0.0) + jnp.log1p(jnp.exp(-jnp.abs(x)))


def _silu(x):
    half = 0.5 * x
    return half + half * jnp.tanh(half)


def _sigmoid(x):
    return 0.5 + 0.5 * jnp.tanh(0.5 * x)


def _gelu_tanh(x):
    return 0.5 * x * (1.0 + jnp.tanh(np.sqrt(2.0 / np.pi).astype(np.float32) * (x + 0.044715 * (x * x * x))))


def _rms(x):
    return x * lax.rsqrt(jnp.mean(x * x, axis=-1, keepdims=True) + RMS_EPS)


def _resident(shape):
    nd = len(shape)
    return pl.BlockSpec(shape, lambda *_: (0,) * nd, pipeline_mode=pl.Buffered(1))


def _params(*sem):
    return pltpu.CompilerParams(dimension_semantics=sem, vmem_limit_bytes=VMEM_LIMIT)


def _mod_kernel(c_ref, w_ref, b_ref, o_ref):
    c = c_ref[...]
    o_ref[...] = jnp.dot(_silu(c), w_ref[...], precision=HIGHEST, preferred_element_type=F32) + b_ref[...]


def _modulation(c, ada_w, ada_b):
    depth, d, n = ada_w.shape
    bsz = c.shape[0]
    rows = -(-bsz // V7X_SUBLANES) * V7X_SUBLANES
    c_pad = jnp.pad(c, ((0, rows - bsz), (0, 0)))
    tn = n // 4
    out = pl.pallas_call(
        _mod_kernel,
        grid=(depth, n // tn),
        in_specs=[pl.BlockSpec((rows, d), lambda l, j: (0, 0)),
                  pl.BlockSpec((None, d, tn), lambda l, j: (l, 0, j)),
                  pl.BlockSpec((None, 1, tn), lambda l, j: (l, 0, j))],
        out_specs=pl.BlockSpec((None, rows, tn), lambda l, j: (l, 0, j)),
        out_shape=jax.ShapeDtypeStruct((depth, rows, n), F32),
        compiler_params=_params("parallel", "parallel"),
        name="adaln_mod",
    )(c_pad, ada_w, ada_b.reshape(depth, 1, n))
    return out[:, :bsz]


def _inproj_kernel(x_ref, nw_ref, sc_ref, sh_ref, w_ref, cw_ref, cb_ref, *refs, pieces, nt):
    o_refs, carry_ref = refs[:len(pieces)], refs[len(pieces)]
    tm = x_ref.shape[0]
    row8 = lax.broadcasted_iota(jnp.int32, (V7X_SUBLANES, nt), 0)

    @pl.when(pl.program_id(1) == 0)
    def _():
        carry_ref[...] = jnp.zeros(carry_ref.shape, F32)

    h = (_rms(x_ref[...]) * nw_ref[...] * (1.0 + sc_ref[...]) + sh_ref[...]).astype(BF16)

    def column_chunk(o_ref, c0, cw, off, coff, conv, bias, act):
        acc = jnp.dot(h, w_ref[:, off + c0:off + c0 + cw], preferred_element_type=F32)
        yield
        if conv:
            j = (coff + c0) // nt
            cs = slice(coff + c0, coff + c0 + cw)
            carried = carry_ref[j]
            carry_ref[j] = acc[tm - V7X_SUBLANES:, :]
            scale = 0.5 if act else 1.0
            taps = [cw_ref[k:k + 1, cs] * scale for k in range(CONV_WIDTH)]
            for r0 in range(0, tm, CONV_ROWS):
                pre = acc[r0:r0 + CONV_ROWS, :]
                prev = carried if r0 == 0 else acc[r0 - V7X_SUBLANES:r0, :]
                out = taps[CONV_WIDTH - 1] * pre
                for shift in range(1, CONV_WIDTH):
                    head = jnp.where(row8 < shift, pltpu.roll(prev, shift, 0),
                                     pltpu.roll(pre[0:V7X_SUBLANES, :], shift, 0))
                    moved = jnp.concatenate([head, pltpu.roll(pre, shift, 0)[V7X_SUBLANES:, :]], axis=0)
                    out = out + taps[CONV_WIDTH - 1 - shift] * moved
                if bias:
                    out = out + cb_ref[:, cs] * scale
                if act:
                    out = out + out * jnp.tanh(out)
                o_ref[r0:r0 + CONV_ROWS, c0:c0 + cw] = out.astype(o_ref.dtype)
        else:
            o_ref[:, c0:c0 + cw] = acc.astype(o_ref.dtype)

    heavy, plain = [], []
    off = coff = 0
    for o_ref, (width, conv, bias, act) in zip(o_refs, pieces):
        for c0 in range(0, width, nt):
            chunk = column_chunk(o_ref, c0, min(nt, width - c0), off, coff, conv, bias, act)
            (heavy if conv else plain).append(chunk)
        off += width
        coff += width if conv else 0
    while heavy or plain:
        for queue in (heavy, plain):
            if queue:
                for _ in queue.pop(0):
                    pass


def _inproj(x, nw, sc, sh, w_all, conv_w, conv_b, pieces, dtypes, tm):
    bsz, seq, d = x.shape
    nt = 512
    conv_width = conv_w.shape[1]
    assert all(p[0] % nt == 0 for p in pieces if p[1])
    tok = lambda width: pl.BlockSpec((None, tm, width), lambda b, i: (b, i, 0))
    per_batch = pl.BlockSpec((None, 1, d), lambda b, i: (b, 0, 0))
    return pl.pallas_call(
        functools.partial(_inproj_kernel, pieces=pieces, nt=nt),
        grid=(bsz, seq // tm),
        in_specs=[tok(d), pl.BlockSpec((1, d), lambda b, i: (0, 0)), per_batch, per_batch,
                  _resident(w_all.shape), pl.BlockSpec((CONV_WIDTH, conv_width), lambda b, i: (0, 0)),
                  pl.BlockSpec((1, conv_width), lambda b, i: (0, 0))],
        out_specs=[tok(p[0]) for p in pieces],
        out_shape=[jax.ShapeDtypeStruct((bsz, seq, p[0]), dt) for p, dt in zip(pieces, dtypes)],
        scratch_shapes=[pltpu.VMEM((conv_width // nt, V7X_SUBLANES, nt), F32)],
        compiler_params=_params("parallel", "arbitrary"),
        name="inproj",
    )(x, nw, sc, sh, w_all, conv_w, conv_b)


def _tri_masks(n):
    row = lax.broadcasted_iota(jnp.int32, (n, n), 0)
    col = lax.broadcasted_iota(jnp.int32, (n, n), 1)
    return row >= col, row > col, row == col


def _masked_exp_diff(mask, col_vec, row_vec):
    return jnp.where(mask, jnp.exp(jnp.where(mask, col_vec - row_vec, 0.0)), 0.0)


def _round_robin(chains):
    active = list(chains)
    while active:
        still = []
        for chain in active:
            try:
                next(chain)
                still.append(chain)
            except StopIteration:
                pass
        active = still


def _dot_exact01(a01, x):
    x1 = x.astype(BF16)
    r1 = x - x1.astype(F32)
    x2 = r1.astype(BF16)
    x3 = (r1 - x2.astype(F32)).astype(BF16)
    return (jnp.dot(a01, x1, preferred_element_type=F32) + jnp.dot(a01, x2, preferred_element_type=F32)
            + jnp.dot(a01, x3, preferred_element_type=F32))


def _dot_exact01_rhs(x, b01):
    x1 = x.astype(BF16)
    r1 = x - x1.astype(F32)
    x2 = r1.astype(BF16)
    x3 = (r1 - x2.astype(F32)).astype(BF16)
    return (jnp.dot(x1, b01, preferred_element_type=F32) + jnp.dot(x2, b01, preferred_element_type=F32)
            + jnp.dot(x3, b01, preferred_element_type=F32))


def _inverse_masks(n):
    row_i = lax.broadcasted_iota(jnp.int32, (n, n), 0)
    col_i = lax.broadcasted_iota(jnp.int32, (n, n), 1)
    masks = [row_i // INV_BASE == col_i // INV_BASE]
    size = INV_BASE
    while size < n:
        masks.append(jnp.logical_and(row_i // (2 * size) == col_i // (2 * size),
                                     jnp.logical_and((row_i // size) % 2 == 1, (col_i // size) % 2 == 0)))
        size *= 2
    return masks


def _gdn_kernel(qkv_ref, z_ref, s_ref, alog_ref, dtb_ref, nw_ref, y_ref, st_ref, *, nheads, dk, dv):
    n = SCAN_CHUNK
    nchunks = qkv_ref.shape[0] // n

    @pl.when(pl.program_id(1) == 0)
    def _():
        st_ref[...] = jnp.zeros(st_ref.shape, F32)

    causal, strict, diag = _tri_masks(n)
    tril_b = causal.astype(BF16)
    eye = diag.astype(F32)
    blk_masks = _inverse_masks(n)
    a_neg = -jnp.exp(alog_ref[...])
    dtb = dtb_ref[...]

    beta_c, gcum_c, gcum_t = [], [], []
    for c in range(nchunks):
        sv = s_ref[c * n:(c + 1) * n, :]
        beta_c.append(jax.nn.sigmoid(sv))
        gcum_c.append(_dot_exact01(tril_b, a_neg * _softplus(sv + dtb)))
        gcum_t.append(gcum_c[-1].T)

    res = {}

    def chain(c, h):
        rows = slice(c * n, (c + 1) * n)
        q = qkv_ref[rows, h * dk:(h + 1) * dk].astype(F32)
        k = qkv_ref[rows, (nheads + h) * dk:(nheads + h + 1) * dk].astype(F32)
        v = qkv_ref[rows, 2 * nheads * dk + h * dv:2 * nheads * dk + (h + 1) * dv].astype(F32)
        q = q * lax.rsqrt(jnp.sum(q * q, axis=-1, keepdims=True) + RMS_EPS) * (dk ** -0.5)
        k = k * lax.rsqrt(jnp.sum(k * k, axis=-1, keepdims=True) + RMS_EPS)
        beta = beta_c[c][:, h:h + 1]
        gc = gcum_c[c][:, nheads + h:nheads + h + 1]
        gr = gcum_t[c][nheads + h:nheads + h + 1, :]
        glast = gcum_c[c][n - 1:n, nheads + h:nheads + h + 1]
        dec = _masked_exp_diff(causal, gc, gr)
        kb = k.astype(BF16)
        kk = lax.dot_general(kb, kb, _NT, preferred_element_type=F32)
        qk = lax.dot_general(q.astype(BF16), kb, _NT, preferred_element_type=F32) * dec
        yield
        m = jnp.where(strict, beta * kk * dec, 0.0)
        nm = -jnp.where(blk_masks[0], m, 0.0)
        inv = eye + nm
        for _ in range(int(np.log2(INV_BASE)) - 1):
            nm_b = nm.astype(BF16)
            nm = jnp.dot(nm_b, nm_b, preferred_element_type=F32)
            yield
            inv = inv + _dotb(inv, nm)
            yield
        for pair_mask in blk_masks[1:]:
            inv_b = inv.astype(BF16)
            low = jnp.dot(jnp.where(pair_mask, m, 0.0).astype(BF16), inv_b, preferred_element_type=F32)
            yield
            inv = inv - jnp.dot(inv_b, low.astype(BF16), preferred_element_type=F32)
            yield
        egc = jnp.exp(gc)
        rhs = jnp.concatenate([beta * v, (beta * egc) * k], axis=1)
        uw = _dotb(inv, rhs).astype(BF16)
        yield
        k_uw = lax.dot_general((k * jnp.exp(glast - gc)).astype(BF16), uw, _TN, preferred_element_type=F32)
        q_uw = jnp.dot(qk.astype(BF16), uw, preferred_element_type=F32)
        yield
        lhs = jnp.concatenate([-k_uw[:, dv:], q * egc - q_uw[:, dv:]], axis=0).astype(BF16)
        res[c, h] = (lhs, k_uw[:, :dv], q_uw[:, :dv], jnp.exp(glast))

    _round_robin([chain(c, h) for c in range(nchunks) for h in range(nheads)])

    states = [st_ref[h] for h in range(nheads)]
    outs = {}
    for c in range(nchunks):
        for h in range(nheads):
            lhs, k_u, q_u, g_tot = res[c, h]
            prod = jnp.dot(lhs, states[h].astype(BF16), preferred_element_type=F32)
            outs[c, h] = prod[dk:] + q_u
            states[h] = states[h] * g_tot + prod[:dk] + k_u
    for h in range(nheads):
        st_ref[h] = states[h]

    for c in range(nchunks):
        rows = slice(c * n, (c + 1) * n)
        for h in range(nheads):
            z = z_ref[rows, h * dv:(h + 1) * dv].astype(F32)
            y_ref[rows, h * dv:(h + 1) * dv] = (_rms(outs[c, h]) * nw_ref[...] * _silu(z)).astype(y_ref.dtype)


def _gdn(p_gdn, p_z, p_s, alog_row, dtb_row, norm_w, nheads, dk, dv, tl):
    bsz, seq, width = p_gdn.shape
    vw = nheads * dv
    row = lambda wd: pl.BlockSpec((1, wd), lambda b, i: (0, 0))
    return pl.pallas_call(
        functools.partial(_gdn_kernel, nheads=nheads, dk=dk, dv=dv),
        grid=(bsz, seq // tl),
        in_specs=[pl.BlockSpec((None, tl, width), lambda b, i: (b, i, 0)),
                  pl.BlockSpec((None, tl, vw), lambda b, i: (b, i, 0)),
                  pl.BlockSpec((None, tl, SMALL_WIDTH), lambda b, i: (b, i, 0)),
                  row(SMALL_WIDTH), row(SMALL_WIDTH), row(dv)],
        out_specs=pl.BlockSpec((None, tl, vw), lambda b, i: (b, i, 0)),
        out_shape=jax.ShapeDtypeStruct((bsz, seq, vw), BF16),
        scratch_shapes=[pltpu.VMEM((nheads, dk, dv), F32)],
        compiler_params=_params("parallel", "arbitrary"),
        name="gdn_mixer",
    )(p_gdn, p_z, p_s, alog_row, dtb_row, norm_w)


def _ssd_kernel(xin_ref, z_ref, s_ref, alog_ref, dtb_ref, dexp_ref, nw_ref, e_ref,
                y_ref, st_ref, yb_ref, *, nheads, hd, ngroups, nstate, lane0):
    n = SCAN_CHUNK
    nchunks = xin_ref.shape[0] // n
    inner = nheads * hd
    hpg = nheads // ngroups
    gw = hpg * hd

    @pl.when(pl.program_id(1) == 0)
    def _():
        st_ref[...] = jnp.zeros(st_ref.shape, F32)

    causal, _, _ = _tri_masks(n)
    tril_b = causal.astype(BF16)
    lane = lax.broadcasted_iota(jnp.int32, (1, SMALL_WIDTH), 1)
    valid = jnp.logical_and(lane >= lane0, lane < lane0 + nheads)
    a_neg = -jnp.exp(alog_ref[...])
    dtb = dtb_ref[...]
    expand = e_ref[...]
    widen = lambda t: jnp.dot(t.astype(BF16), expand, preferred_element_type=F32)

    acum_c, acum_t, ea_c, cd_c, xdt_c, xw_c = [], [], [], [], [], []
    for c in range(nchunks):
        rows = slice(c * n, (c + 1) * n)
        dt_all = jnp.where(valid, _softplus(s_ref[rows, :] + dtb), 0.0)
        acum = _dot_exact01(tril_b, dt_all * a_neg)
        alast = acum[n - 1:n, :]
        acum_c.append(acum)
        acum_t.append(acum.T)
        ea_c.append(widen(jnp.exp(acum)))
        cd_c.append(_dot_exact01_rhs(jnp.broadcast_to(jnp.exp(alast), (V7X_SUBLANES, SMALL_WIDTH)),
                                     expand)[0:1, :])
        xdt = xin_ref[rows, :inner].astype(F32) * widen(dt_all)
        xdt_c.append(xdt.astype(BF16))
        xw_c.append((xdt * widen(jnp.exp(alast - acum))).astype(BF16))

    grown = {}

    def chain(c, g):
        rows = slice(c * n, (c + 1) * n)
        sl = slice(g * gw, (g + 1) * gw)
        bg = xin_ref[rows, inner + g * nstate:inner + (g + 1) * nstate]
        cg = xin_ref[rows, inner + (ngroups + g) * nstate:inner + (ngroups + g + 1) * nstate]
        cb = lax.dot_general(cg, bg, _NT, preferred_element_type=F32)
        grown[c, g] = lax.dot_general(bg, xw_c[c][:, sl], _TN, preferred_element_type=F32)
        yield
        for hh in range(hpg):
            h = g * hpg + hh
            ln = lane0 + h
            lmat = _masked_exp_diff(causal, acum_c[c][:, ln:ln + 1], acum_t[c][ln:ln + 1, :])
            yb_ref[rows, h * hd:(h + 1) * hd] = jnp.dot((cb * lmat).astype(BF16), xdt_c[c][:, h * hd:(h + 1) * hd],
                                                        preferred_element_type=F32)
            yield

    _round_robin([chain(c, g) for c in range(nchunks) for g in range(ngroups)])

    for g in range(ngroups):
        sl = slice(g * gw, (g + 1) * gw)
        state = st_ref[g]
        for c in range(nchunks):
            rows = slice(c * n, (c + 1) * n)
            cg = xin_ref[rows, inner + (ngroups + g) * nstate:inner + (ngroups + g + 1) * nstate]
            y_off = jnp.dot(cg, state.astype(BF16), preferred_element_type=F32) * ea_c[c][:, sl]
            yb_ref[rows, sl] = yb_ref[rows, sl] + y_off
            state = state * cd_c[c][:, sl] + grown[c, g]
        st_ref[g] = state

    for c in range(nchunks):
        rows = slice(c * n, (c + 1) * n)
        y = yb_ref[rows, :] + dexp_ref[...] * xin_ref[rows, :inner].astype(F32)
        gz = y * _silu(z_ref[rows, :].astype(F32))
        for g in range(ngroups):
            sl = slice(g * gw, (g + 1) * gw)
            y_ref[rows, sl] = (_rms(gz[:, sl]) * nw_ref[:, sl]).astype(y_ref.dtype)


def _ssd(p_ssd, p_z, p_s, alog_row, dtb_row, d_exp, norm_w, expand, nheads, hd, ngroups, nstate, lane0, n):
    bsz, seq, width = p_ssd.shape
    inner = nheads * hd
    row = lambda wd: pl.BlockSpec((1, wd), lambda b, i: (0, 0))
    return pl.pallas_call(
        functools.partial(_ssd_kernel, nheads=nheads, hd=hd, ngroups=ngroups, nstate=nstate, lane0=lane0),
        grid=(bsz, seq // n),
        in_specs=[pl.BlockSpec((None, n, width), lambda b, i: (b, i, 0)),
                  pl.BlockSpec((None, n, inner), lambda b, i: (b, i, 1)),
                  pl.BlockSpec((None, n, SMALL_WIDTH), lambda b, i: (b, i, 0)),
                  row(SMALL_WIDTH), row(SMALL_WIDTH), row(inner), row(inner),
                  pl.BlockSpec((SMALL_WIDTH, inner), lambda b, i: (0, 0))],
        out_specs=pl.BlockSpec((None, n, inner), lambda b, i: (b, i, 0)),
        out_shape=jax.ShapeDtypeStruct((bsz, seq, inner), BF16),
        scratch_shapes=[pltpu.VMEM((ngroups, nstate, inner // ngroups), F32),
                        pltpu.VMEM((n, inner), F32)],
        compiler_params=_params("parallel", "arbitrary"),
        name="ssd_mixer",
    )(p_ssd, p_z, p_s, alog_row, dtb_row, d_exp, norm_w, expand)


def _lru_kernel(xin_ref, gate_ref, wax_ref, bax_ref, lam_ref, y_ref, h_ref):
    n, width = xin_ref.shape

    @pl.when(pl.program_id(1) == 0)
    def _():
        h_ref[...] = jnp.zeros(h_ref.shape, F32)

    xc = xin_ref[...].astype(F32)
    rx = jnp.dot(xin_ref[...], wax_ref[...], preferred_element_type=F32) + bax_ref[...]
    r = _sigmoid(rx[:, :width])
    ig = _sigmoid(rx[:, width:])
    a = jnp.exp(-LRU_C * r * _softplus(-lam_ref[...]))
    y = 1.0 - a * a
    u = jnp.where(y > 0.0, y * lax.rsqrt(y), 0.0) * (ig * xc)

    groups = n // V7X_SUBLANES
    a = a.reshape(groups, V7X_SUBLANES, width)
    u = u.reshape(groups, V7X_SUBLANES, width)
    sub = lax.broadcasted_iota(jnp.int32, (groups, V7X_SUBLANES, width), 1)
    d = 1
    while d < V7X_SUBLANES:
        keep = sub >= d
        u = jnp.where(keep, a * pltpu.roll(u, d, 1) + u, u)
        a = jnp.where(keep, a * pltpu.roll(a, d, 1), a)
        d *= 2
    h = h_ref[...]
    hs = []
    for g in range(groups):
        hs.append(a[g] * h + u[g])
        h = hs[-1][V7X_SUBLANES - 1:V7X_SUBLANES, :]
    h_ref[...] = h
    hs = jnp.concatenate(hs, axis=0)
    y_ref[...] = (hs * _gelu_tanh(gate_ref[...].astype(F32))).astype(y_ref.dtype)


def _lru(p_lru, p_z, wax, bax, lam, tl):
    bsz, seq, width = p_lru.shape
    row = lambda wd: pl.BlockSpec((1, wd), lambda b, i: (0, 0))
    return pl.pallas_call(
        _lru_kernel,
        grid=(bsz, seq // tl),
        in_specs=[pl.BlockSpec((None, tl, width), lambda b, i: (b, i, 0)),
                  pl.BlockSpec((None, tl, width), lambda b, i: (b, i, 2)),
                  pl.BlockSpec((width, 2 * width), lambda b, i: (0, 0)), row(2 * width), row(width)],
        out_specs=pl.BlockSpec((None, tl, width), lambda b, i: (b, i, 0)),
        out_shape=jax.ShapeDtypeStruct((bsz, seq, width), BF16),
        scratch_shapes=[pltpu.VMEM((1, width), F32)],
        compiler_params=_params("parallel", "arbitrary"),
        name="lru_mixer",
    )(p_lru, p_z, wax, bax, lam)


def _merge_mlp_kernel(ya_ref, yb_ref, yc_ref, gl_ref, x_ref, gt1_ref, nw_ref, sc_ref, sh_ref, gt2_ref,
                      wb_ref, wo_ref, wup_ref, wdn_ref, fnw_ref, o_ref, *, final, fc):
    d = x_ref.shape[1]
    merged = None
    for r, y_ref in enumerate((ya_ref, yb_ref, yc_ref)):
        br = jnp.dot(y_ref[...], wb_ref[r], preferred_element_type=F32)
        term = jax.nn.sigmoid(gl_ref[:, r * d:(r + 1) * d].astype(F32)) * br
        merged = term if merged is None else merged + term
    mix = jnp.dot(merged.astype(BF16), wo_ref[...], preferred_element_type=F32)
    x1 = x_ref[...] + gt1_ref[...] * mix

    h = (_rms(x1) * nw_ref[...] * (1.0 + sc_ref[...]) + sh_ref[...]).astype(BF16)
    acc = None
    for c0 in range(0, wup_ref.shape[1], fc):
        act = jnp.maximum(jnp.dot(h, wup_ref[:, c0:c0 + fc], preferred_element_type=F32), 0.0)
        part = jnp.dot((act * act).astype(BF16), wdn_ref[c0:c0 + fc, :], preferred_element_type=F32)
        acc = part if acc is None else acc + part
    x2 = x1 + gt2_ref[...] * acc
    if final:
        x2 = _rms(x2) * fnw_ref[...]
    o_ref[...] = x2


def _merge_mlp(ya, yb, yc, p_gl, x, gt1, nw, sc, sh, gt2, wb, wo, wup, wdn, fnw, final, tm):
    bsz, seq, d = x.shape
    bw = ya.shape[2]
    tok = lambda width: pl.BlockSpec((None, tm, width), lambda b, i: (b, i, 0))
    per_batch = pl.BlockSpec((None, 1, d), lambda b, i: (b, 0, 0))
    row = pl.BlockSpec((1, d), lambda b, i: (0, 0))
    return pl.pallas_call(
        functools.partial(_merge_mlp_kernel, final=final, fc=1024),
        grid=(bsz, seq // tm),
        in_specs=[tok(bw), tok(bw), tok(bw), tok(N_BRANCH * d), tok(d), per_batch, row, per_batch, per_batch,
                  per_batch, _resident(wb.shape), _resident(wo.shape), _resident(wup.shape),
                  _resident(wdn.shape), row],
        out_specs=tok(d),
        out_shape=jax.ShapeDtypeStruct((bsz, seq, d), F32),
        compiler_params=_params("parallel", "parallel"),
        name="merge_mlp",
    )(ya, yb, yc, p_gl, x, gt1, nw, sc, sh, gt2, wb, wo, wup, wdn, fnw)


def _lane_row(vec, lane0):
    return jnp.pad(vec.astype(F32), (lane0, SMALL_WIDTH - lane0 - vec.shape[0])).reshape(1, SMALL_WIDTH)


def _block_diag(w):
    nb, a, b = w.shape
    eye = jnp.eye(nb, dtype=w.dtype)
    return (eye[:, None, :, None] * w[:, :, None, :]).reshape(nb * a, nb * b)


def kernel(x, c, ada_w, ada_b, norm_mix, w_in, gdn_conv_w, gdn_a_log, gdn_dt_bias, gdn_norm, ssd_conv_w, ssd_conv_b, ssd_a_log, ssd_dt_bias, ssd_d, ssd_norm, lru_conv_w, lru_conv_b, lru_w_a, lru_b_a, lru_w_x, lru_b_x, lru_lambda, w_branch, w_out, norm_mlp, w_up, w_down, final_norm):
    bsz, seq, d = x.shape
    depth = ada_w.shape[0]
    gh = gdn_a_log.shape[1]
    dv = gdn_norm.shape[1]
    gqk = (gdn_conv_w.shape[2] - gh * dv) // 2
    dk = gqk // gh
    sh_ = ssd_a_log.shape[1]
    inner = ssd_norm.shape[1]
    hd = inner // sh_
    sbc = (ssd_conv_w.shape[2] - inner) // 2
    ngroups = SSD_GROUPS
    nstate = sbc // ngroups
    lw = lru_lambda.shape[1]
    assert seq % MIXER_BLOCK == 0 and MIXER_BLOCK % SCAN_CHUNK == 0 and 2 * gh + sh_ <= SMALL_WIDTH
    assert seq % SCAN_BLOCK == 0 and SCAN_BLOCK % SCAN_CHUNK == 0

    splits = (gqk, gqk, gh * dv, gh * dv, gh, gh, inner, inner, sbc, sbc, sh_, lw, lw, N_BRANCH * d)
    offs = np.concatenate([[0], np.cumsum(splits)])
    cols = lambda w, i: w[:, offs[i]:offs[i + 1]]
    ssd_lane0 = 2 * gh
    widths = (2 * gqk + gh * dv, inner + 2 * sbc, lw, gh * dv + inner + lw, N_BRANCH * d, SMALL_WIDTH)
    dtypes = (BF16, BF16, BF16, BF16, BF16, F32)
    pieces = ((widths[0], True, False, True), (widths[1], True, True, True), (widths[2], True, True, False),
              (widths[3], False, False, False), (widths[4], False, False, False), (widths[5], False, False, False))

    mod = _modulation(c, ada_w, ada_b)
    expand = jnp.asarray(np.kron(np.eye(SMALL_WIDTH, sh_, -ssd_lane0), np.ones((1, hd))), BF16)

    tm = 512
    for l in range(depth):
        sh1, sc1, gt1, sh2, sc2, gt2 = [mod[l, :, None, j * d:(j + 1) * d] for j in range(N_MOD)]
        w = w_in[l]
        small = jnp.concatenate([cols(w, 4), cols(w, 5), cols(w, 10)], axis=1)
        w_all = jnp.concatenate(
            [cols(w, 0), cols(w, 1), cols(w, 2),
             cols(w, 6), cols(w, 8), cols(w, 9),
             cols(w, 11),
             cols(w, 3), cols(w, 7), cols(w, 12),
             cols(w, 13),
             jnp.pad(small, ((0, 0), (0, SMALL_WIDTH - small.shape[1])))], axis=1).astype(BF16)
        conv_w = jnp.concatenate([gdn_conv_w[l], ssd_conv_w[l], lru_conv_w[l]], axis=1)
        conv_b = jnp.concatenate([jnp.zeros((widths[0],), F32), ssd_conv_b[l], lru_conv_b[l]]).reshape(1, -1)
        p_gdn, p_ssd, p_lru, p_z, p_gl, p_s = _inproj(
            x, norm_mix[l].reshape(1, d), sc1, sh1, w_all, conv_w, conv_b, pieces, dtypes, tm)

        y_a = _gdn(p_gdn, p_z, p_s, _lane_row(gdn_a_log[l], gh), _lane_row(gdn_dt_bias[l], gh),
                   gdn_norm[l].reshape(1, dv), gh, dk, dv, MIXER_BLOCK)
        y_b = _ssd(p_ssd, p_z, p_s, _lane_row(ssd_a_log[l], ssd_lane0), _lane_row(ssd_dt_bias[l], ssd_lane0),
                   jnp.repeat(ssd_d[l], hd).reshape(1, inner), ssd_norm[l].reshape(1, inner), expand,
                   sh_, hd, ngroups, nstate, ssd_lane0, SCAN_BLOCK)
        wax = jnp.concatenate([_block_diag(lru_w_a[l]), _block_diag(lru_w_x[l])], axis=1).astype(BF16)
        bax = jnp.concatenate([lru_b_a[l], lru_b_x[l]]).reshape(1, 2 * lw)
        y_c = _lru(p_lru, p_z, wax, bax, lru_lambda[l].reshape(1, lw), SCAN_BLOCK)

        x = _merge_mlp(y_a, y_b, y_c, p_gl, x, gt1, norm_mlp[l].reshape(1, d), sc2, sh2, gt2,
                       w_branch[l].astype(BF16), w_out[l].astype(BF16), w_up[l].astype(BF16),
                       w_down[l].astype(BF16), final_norm.reshape(1, d), l == depth - 1, tm)
    return x
```

```python
import functools

import numpy as np
import jax
import jax.numpy as jnp
from jax import lax
from jax.experimental import pallas as pl
from jax.experimental.pallas import tpu as pltpu

F32 = jnp.float32
BF16 = jnp.bfloat16
HIGHEST = lax.Precision.HIGHEST

RMS_EPS = 1e-6
CONV_WIDTH = 4
LRU_C = 8.0
N_MOD = 6
N_BRANCH = 3

V7X_LANES = 128
V7X_SUBLANES = 8
SCAN_CHUNK = 128
MIXER_BLOCK = 1024
SCAN_BLOCK = 2048
CONV_ROWS = 512
INV_BASE = 16
SSD_GROUPS = 2
SMALL_WIDTH = V7X_LANES
VMEM_LIMIT = 56 * 1024 * 1024

_NT = (((1,), (1,)), ((), ()))
_TN = (((0,), (0,)), ((), ()))


def _dotb(a, b):
    return jnp.dot(a.astype(BF16), b.astype(BF16), preferred_element_type=F32)


def _softplus(x):
    return jnp.maximum(x, 0.0) + jnp.log1p(jnp.exp(-jnp.abs(x)))


def _silu(x):
    half = 0.5 * x
    return half + half * jnp.tanh(half)


def _sigmoid(x):
    return 0.5 + 0.5 * jnp.tanh(0.5 * x)


def _gelu_tanh(x):
    return 0.5 * x * (1.0 + jnp.tanh(np.sqrt(2.0 / np.pi).astype(np.float32) * (x + 0.044715 * (x * x * x))))


def _rms(x):
    return x * lax.rsqrt(jnp.mean(x * x, axis=-1, keepdims=True) + RMS_EPS)


def _resident(shape):
    nd = len(shape)
    return pl.BlockSpec(shape, lambda *_: (0,) * nd, pipeline_mode=pl.Buffered(1))


def _params(*sem):
    return pltpu.CompilerParams(dimension_semantics=sem, vmem_limit_bytes=VMEM_LIMIT)


def _mod_kernel(c_ref, w_ref, b_ref, o_ref):
    c = c_ref[...]
    o_ref[...] = jnp.dot(_silu(c), w_ref[...], precision=HIGHEST, preferred_element_type=F32) + b_ref[...]


def _modulation(c, ada_w, ada_b):
    depth, d, n = ada_w.shape
    bsz = c.shape[0]
    rows = -(-bsz // V7X_SUBLANES) * V7X_SUBLANES
    c_pad = jnp.pad(c, ((0, rows - bsz), (0, 0)))
    tn = n // 4
    out = pl.pallas_call(
        _mod_kernel,
        grid=(depth, n // tn),
        in_specs=[pl.BlockSpec((rows, d), lambda l, j: (0, 0)),
                  pl.BlockSpec((None, d, tn), lambda l, j: (l, 0, j)),
                  pl.BlockSpec((None, 1, tn), lambda l, j: (l, 0, j))],
        out_specs=pl.BlockSpec((None, rows, tn), lambda l, j: (l, 0, j)),
        out_shape=jax.ShapeDtypeStruct((depth, rows, n), F32),
        compiler_params=_params("parallel", "parallel"),
        name="adaln_mod",
    )(c_pad, ada_w, ada_b.reshape(depth, 1, n))
    return out[:, :bsz]


def _inproj_kernel(x_ref, nw_ref, sc_ref, sh_ref, w_ref, cw_ref, cb_ref, *refs, pieces, nt):
    o_refs, carry_ref = refs[:len(pieces)], refs[len(pieces)]
    tm = x_ref.shape[0]
    row8 = lax.broadcasted_iota(jnp.int32, (V7X_SUBLANES, nt), 0)

    @pl.when(pl.program_id(1) == 0)
    def _():
        carry_ref[...] = jnp.zeros(carry_ref.shape, F32)

    h = (_rms(x_ref[...]) * nw_ref[...] * (1.0 + sc_ref[...]) + sh_ref[...]).astype(BF16)

    def column_chunk(o_ref, c0, cw, off, coff, conv, bias, act):
        acc = jnp.dot(h, w_ref[:, off + c0:off + c0 + cw], preferred_element_type=F32)
        yield
        if conv:
            j = (coff + c0) // nt
            cs = slice(coff + c0, coff + c0 + cw)
            carried = carry_ref[j]
            carry_ref[j] = acc[tm - V7X_SUBLANES:, :]
            scale = 0.5 if act else 1.0
            taps = [cw_ref[k:k + 1, cs] * scale for k in range(CONV_WIDTH)]
            for r0 in range(0, tm, CONV_ROWS):
                pre = acc[r0:r0 + CONV_ROWS, :]
                prev = carried if r0 == 0 else acc[r0 - V7X_SUBLANES:r0, :]
                out = taps[CONV_WIDTH - 1] * pre
                for shift in range(1, CONV_WIDTH):
                    head = jnp.where(row8 < shift, pltpu.roll(prev, shift, 0),
                                     pltpu.roll(pre[0:V7X_SUBLANES, :], shift, 0))
                    moved = jnp.concatenate([head, pltpu.roll(pre, shift, 0)[V7X_SUBLANES:, :]], axis=0)
                    out = out + taps[CONV_WIDTH - 1 - shift] * moved
                if bias:
                    out = out + cb_ref[:, cs] * scale
                if act:
                    out = out + out * jnp.tanh(out)
                o_ref[r0:r0 + CONV_ROWS, c0:c0 + cw] = out.astype(o_ref.dtype)
        else:
            o_ref[:, c0:c0 + cw] = acc.astype(o_ref.dtype)

    heavy, plain = [], []
    off = coff = 0
    for o_ref, (width, conv, bias, act) in zip(o_refs, pieces):
        for c0 in range(0, width, nt):
            chunk = column_chunk(o_ref, c0, min(nt, width - c0), off, coff, conv, bias, act)
            (heavy if conv else plain).append(chunk)
        off += width
        coff += width if conv else 0
    while heavy or plain:
        for queue in (heavy, plain):
            if queue:
                for _ in queue.pop(0):
                    pass


def _inproj(x, nw, sc, sh, w_all, conv_w, conv_b, pieces, dtypes, tm):
    bsz, seq, d = x.shape
    nt = 512
    conv_width = conv_w.shape[1]
    assert all(p[0] % nt == 0 for p in pieces if p[1])
    tok = lambda width: pl.BlockSpec((None, tm, width), lambda b, i: (b, i, 0))
    per_batch = pl.BlockSpec((None, 1, d), lambda b, i: (b, 0, 0))
    return pl.pallas_call(
        functools.partial(_inproj_kernel, pieces=pieces, nt=nt),
        grid=(bsz, seq // tm),
        in_specs=[tok(d), pl.BlockSpec((1, d), lambda b, i: (0, 0)), per_batch, per_batch,
                  _resident(w_all.shape), pl.BlockSpec((CONV_WIDTH, conv_width), lambda b, i: (0, 0)),
                  pl.BlockSpec((1, conv_width), lambda b, i: (0, 0))],
        out_specs=[tok(p[0]) for p in pieces],
        out_shape=[jax.ShapeDtypeStruct((bsz, seq, p[0]), dt) for p, dt in zip(pieces, dtypes)],
        scratch_shapes=[pltpu.VMEM((conv_width // nt, V7X_SUBLANES, nt), F32)],
        compiler_params=_params("parallel", "arbitrary"),
        name="inproj",
    )(x, nw, sc, sh, w_all, conv_w, conv_b)


def _tri_masks(n):
    row = lax.broadcasted_iota(jnp.int32, (n, n), 0)
    col = lax.broadcasted_iota(jnp.int32, (n, n), 1)
    return row >= col, row > col, row == col


def _masked_exp_diff(mask, col_vec, row_vec):
    return jnp.where(mask, jnp.exp(jnp.where(mask, col_vec - row_vec, 0.0)), 0.0)


def _round_robin(chains):
    active = list(chains)
    while active:
        still = []
        for chain in active:
            try:
                next(chain)
                still.append(chain)
            except StopIteration:
                pass
        active = still


def _dot_exact01(a01, x):
    x1 = x.astype(BF16)
    r1 = x - x1.astype(F32)
    x2 = r1.astype(BF16)
    x3 = (r1 - x2.astype(F32)).astype(BF16)
    return (jnp.dot(a01, x1, preferred_element_type=F32) + jnp.dot(a01, x2, preferred_element_type=F32)
            + jnp.dot(a01, x3, preferred_element_type=F32))


def _dot_exact01_rhs(x, b01):
    x1 = x.astype(BF16)
    r1 = x - x1.astype(F32)
    x2 = r1.astype(BF16)
    x3 = (r1 - x2.astype(F32)).astype(BF16)
    return (jnp.dot(x1, b01, preferred_element_type=F32) + jnp.dot(x2, b01, preferred_element_type=F32)
            + jnp.dot(x3, b01, preferred_element_type=F32))


def _inverse_masks(n):
    row_i = lax.broadcasted_iota(jnp.int32, (n, n), 0)
    col_i = lax.broadcasted_iota(jnp.int32, (n, n), 1)
    masks = [row_i // INV_BASE == col_i // INV_BASE]
    size = INV_BASE
    while size < n:
        masks.append(jnp.logical_and(row_i // (2 * size) == col_i // (2 * size),
                                     jnp.logical_and((row_i // size) % 2 == 1, (col_i // size) % 2 == 0)))
        size *= 2
    return masks


def _gdn_kernel(qkv_ref, z_ref, s_ref, alog_ref, dtb_ref, nw_ref, y_ref, st_ref, *, nheads, dk, dv):
    n = SCAN_CHUNK
    nchunks = qkv_ref.shape[0] // n

    @pl.when(pl.program_id(1) == 0)
    def _():
        st_ref[...] = jnp.zeros(st_ref.shape, F32)

    causal, strict, diag = _tri_masks(n)
    tril_b = causal.astype(BF16)
    eye = diag.astype(F32)
    blk_masks = _inverse_masks(n)
    a_neg = -jnp.exp(alog_ref[...])
    dtb = dtb_ref[...]

    beta_c, gcum_c, gcum_t = [], [], []
    for c in range(nchunks):
        sv = s_ref[c * n:(c + 1) * n, :]
        beta_c.append(jax.nn.sigmoid(sv))
        gcum_c.append(_dot_exact01(tril_b, a_neg * _softplus(sv + dtb)))
        gcum_t.append(gcum_c[-1].T)

    res = {}

    def chain(c, h):
        rows = slice(c * n, (c + 1) * n)
        q = qkv_ref[rows, h * dk:(h + 1) * dk].astype(F32)
        k = qkv_ref[rows, (nheads + h) * dk:(nheads + h + 1) * dk].astype(F32)
        v = qkv_ref[rows, 2 * nheads * dk + h * dv:2 * nheads * dk + (h + 1) * dv].astype(F32)
        q = q * lax.rsqrt(jnp.sum(q * q, axis=-1, keepdims=True) + RMS_EPS) * (dk ** -0.5)
        k = k * lax.rsqrt(jnp.sum(k * k, axis=-1, keepdims=True) + RMS_EPS)
        beta = beta_c[c][:, h:h + 1]
        gc = gcum_c[c][:, nheads + h:nheads + h + 1]
        gr = gcum_t[c][nheads + h:nheads + h + 1, :]
        glast = gcum_c[c][n - 1:n, nheads + h:nheads + h + 1]
        dec = _masked_exp_diff(causal, gc, gr)
        kb = k.astype(BF16)
        kk = lax.dot_general(kb, kb, _NT, preferred_element_type=F32)
        qk = lax.dot_general(q.astype(BF16), kb, _NT, preferred_element_type=F32) * dec
        yield
        m = jnp.where(strict, beta * kk * dec, 0.0)
        nm = -jnp.where(blk_masks[0], m, 0.0)
        inv = eye + nm
        for _ in range(int(np.log2(INV_BASE)) - 1):
            nm_b = nm.astype(BF16)
            nm = jnp.dot(nm_b, nm_b, preferred_element_type=F32)
            yield
            inv = inv + _dotb(inv, nm)
            yield
        for pair_mask in blk_masks[1:]:
            inv_b = inv.astype(BF16)
            low = jnp.dot(jnp.where(pair_mask, m, 0.0).astype(BF16), inv_b, preferred_element_type=F32)
            yield
            inv = inv - jnp.dot(inv_b, low.astype(BF16), preferred_element_type=F32)
            yield
        egc = jnp.exp(gc)
        rhs = jnp.concatenate([beta * v, (beta * egc) * k], axis=1)
        uw = _dotb(inv, rhs).astype(BF16)
        yield
        k_uw = lax.dot_general((k * jnp.exp(glast - gc)).astype(BF16), uw, _TN, preferred_element_type=F32)
        q_uw = jnp.dot(qk.astype(BF16), uw, preferred_element_type=F32)
        yield
        lhs = jnp.concatenate([-k_uw[:, dv:], q * egc - q_uw[:, dv:]], axis=0).astype(BF16)
        res[c, h] = (lhs, k_uw[:, :dv], q_uw[:, :dv], jnp.exp(glast))

    _round_robin([chain(c, h) for c in range(nchunks) for h in range(nheads)])

    states = [st_ref[h] for h in range(nheads)]
    outs = {}
    for c in range(nchunks):
        for h in range(nheads):
            lhs, k_u, q_u, g_tot = res[c, h]
            prod = jnp.dot(lhs, states[h].astype(BF16), preferred_element_type=F32)
            outs[c, h] = prod[dk:] + q_u
            states[h] = states[h] * g_tot + prod[:dk] + k_u
    for h in range(nheads):
        st_ref[h] = states[h]

    for c in range(nchunks):
        rows = slice(c * n, (c + 1) * n)
        for h in range(nheads):
            z = z_ref[rows, h * dv:(h + 1) * dv].astype(F32)
            y_ref[rows, h * dv:(h + 1) * dv] = (_rms(outs[c, h]) * nw_ref[...] * _silu(z)).astype(y_ref.dtype)


def _gdn(p_gdn, p_z, p_s, alog_row, dtb_row, norm_w, nheads, dk, dv, tl):
    bsz, seq, width = p_gdn.shape
    vw = nheads * dv
    row = lambda wd: pl.BlockSpec((1, wd), lambda b, i: (0, 0))
    return pl.pallas_call(
        functools.partial(_gdn_kernel, nheads=nheads, dk=dk, dv=dv),
        grid=(bsz, seq // tl),
        in_specs=[pl.BlockSpec((None, tl, width), lambda b, i: (b, i, 0)),
                  pl.BlockSpec((None, tl, vw), lambda b, i: (b, i, 0)),
                  pl.BlockSpec((None, tl, SMALL_WIDTH), lambda b, i: (b, i, 0)),
                  row(SMALL_WIDTH), row(SMALL_WIDTH), row(dv)],
        out_specs=pl.BlockSpec((None, tl, vw), lambda b, i: (b, i, 0)),
        out_shape=jax.ShapeDtypeStruct((bsz, seq, vw), BF16),
        scratch_shapes=[pltpu.VMEM((nheads, dk, dv), F32)],
        compiler_params=_params("parallel", "arbitrary"),
        name="gdn_mixer",
    )(p_gdn, p_z, p_s, alog_row, dtb_row, norm_w)


def _ssd_kernel(xin_ref, z_ref, s_ref, alog_ref, dtb_ref, dexp_ref, nw_ref, e_ref,
                y_ref, st_ref, yb_ref, *, nheads, hd, ngroups, nstate, lane0):
    n = SCAN_CHUNK
    nchunks = xin_ref.shape[0] // n
    inner = nheads * hd
    hpg = nheads // ngroups
    gw = hpg * hd

    @pl.when(pl.program_id(1) == 0)
    def _():
        st_ref[...] = jnp.zeros(st_ref.shape, F32)

    causal, _, _ = _tri_masks(n)
    tril_b = causal.astype(BF16)
    lane = lax.broadcasted_iota(jnp.int32, (1, SMALL_WIDTH), 1)
    valid = jnp.logical_and(lane >= lane0, lane < lane0 + nheads)
    a_neg = -jnp.exp(alog_ref[...])
    dtb = dtb_ref[...]
    expand = e_ref[...]
    widen = lambda t: jnp.dot(t.astype(BF16), expand, preferred_element_type=F32)

    acum_c, acum_t, ea_c, cd_c, xdt_c, xw_c = [], [], [], [], [], []
    for c in range(nchunks):
        rows = slice(c * n, (c + 1) * n)
        dt_all = jnp.where(valid, _softplus(s_ref[rows, :] + dtb), 0.0)
        acum = _dot_exact01(tril_b, dt_all * a_neg)
        alast = acum[n - 1:n, :]
        acum_c.append(acum)
        acum_t.append(acum.T)
        ea_c.append(widen(jnp.exp(acum)))
        cd_c.append(_dot_exact01_rhs(jnp.broadcast_to(jnp.exp(alast), (V7X_SUBLANES, SMALL_WIDTH)),
                                     expand)[0:1, :])
        xdt = xin_ref[rows, :inner].astype(F32) * widen(dt_all)
        xdt_c.append(xdt.astype(BF16))
        xw_c.append((xdt * widen(jnp.exp(alast - acum))).astype(BF16))

    grown = {}

    def chain(c, g):
        rows = slice(c * n, (c + 1) * n)
        sl = slice(g * gw, (g + 1) * gw)
        bg = xin_ref[rows, inner + g * nstate:inner + (g + 1) * nstate]
        cg = xin_ref[rows, inner + (ngroups + g) * nstate:inner + (ngroups + g + 1) * nstate]
        cb = lax.dot_general(cg, bg, _NT, preferred_element_type=F32)
        grown[c, g] = lax.dot_general(bg, xw_c[c][:, sl], _TN, preferred_element_type=F32)
        yield
        for hh in range(hpg):
            h = g * hpg + hh
            ln = lane0 + h
            lmat = _masked_exp_diff(causal, acum_c[c][:, ln:ln + 1], acum_t[c][ln:ln + 1, :])
            yb_ref[rows, h * hd:(h + 1) * hd] = jnp.dot((cb * lmat).astype(BF16), xdt_c[c][:, h * hd:(h + 1) * hd],
                                                        preferred_element_type=F32)
            yield

    _round_robin([chain(c, g) for c in range(nchunks) for g in range(ngroups)])

    for g in range(ngroups):
        sl = slice(g * gw, (g + 1) * gw)
        state = st_ref[g]
        for c in range(nchunks):
            rows = slice(c * n, (c + 1) * n)
            cg = xin_ref[rows, inner + (ngroups + g) * nstate:inner + (ngroups + g + 1) * nstate]
            y_off = jnp.dot(cg, state.astype(BF16), preferred_element_type=F32) * ea_c[c][:, sl]
            yb_ref[rows, sl] = yb_ref[rows, sl] + y_off
            state = state * cd_c[c][:, sl] + grown[c, g]
        st_ref[g] = state

    for c in range(nchunks):
        rows = slice(c * n, (c + 1) * n)
        y = yb_ref[rows, :] + dexp_ref[...] * xin_ref[rows, :inner].astype(F32)
        gz = y * _silu(z_ref[rows, :].astype(F32))
        for g in range(ngroups):
            sl = slice(g * gw, (g + 1) * gw)
            y_ref[rows, sl] = (_rms(gz[:, sl]) * nw_ref[:, sl]).astype(y_ref.dtype)


def _ssd(p_ssd, p_z, p_s, alog_row, dtb_row, d_exp, norm_w, expand, nheads, hd, ngroups, nstate, lane0, n):
    bsz, seq, width = p_ssd.shape
    inner = nheads * hd
    row = lambda wd: pl.BlockSpec((1, wd), lambda b, i: (0, 0))
    return pl.pallas_call(
        functools.partial(_ssd_kernel, nheads=nheads, hd=hd, ngroups=ngroups, nstate=nstate, lane0=lane0),
        grid=(bsz, seq // n),
        in_specs=[pl.BlockSpec((None, n, width), lambda b, i: (b, i, 0)),
                  pl.BlockSpec((None, n, inner), lambda b, i: (b, i, 1)),
                  pl.BlockSpec((None, n, SMALL_WIDTH), lambda b, i: (b, i, 0)),
                  row(SMALL_WIDTH), row(SMALL_WIDTH), row(inner), row(inner),
                  pl.BlockSpec((SMALL_WIDTH, inner), lambda b, i: (0, 0))],
        out_specs=pl.BlockSpec((None, n, inner), lambda b, i: (b, i, 0)),
        out_shape=jax.ShapeDtypeStruct((bsz, seq, inner), BF16),
        scratch_shapes=[pltpu.VMEM((ngroups, nstate, inner // ngroups), F32),
                        pltpu.VMEM((n, inner), F32)],
        compiler_params=_params("parallel", "arbitrary"),
        name="ssd_mixer",
    )(p_ssd, p_z, p_s, alog_row, dtb_row, d_exp, norm_w, expand)


def _lru_block(xin_ref, gate_ref, wax_ref, bax_ref, lam_ref, h_ref):
    n, width = xin_ref.shape
    xc = xin_ref[...].astype(F32)
    rx = jnp.dot(xin_ref[...], wax_ref[...], preferred_element_type=F32) + bax_ref[...]
    r = _sigmoid(rx[:, :width])
    ig = _sigmoid(rx[:, width:])
    a = jnp.exp(-LRU_C * r * _softplus(-lam_ref[...]))
    y = 1.0 - a * a
    u = jnp.where(y > 0.0, y * lax.rsqrt(y), 0.0) * (ig * xc)

    groups = n // V7X_SUBLANES
    a = a.reshape(groups, V7X_SUBLANES, width)
    u = u.reshape(groups, V7X_SUBLANES, width)
    sub = lax.broadcasted_iota(jnp.int32, (groups, V7X_SUBLANES, width), 1)
    d = 1
    while d < V7X_SUBLANES:
        keep = sub >= d
        u = jnp.where(keep, a * pltpu.roll(u, d, 1) + u, u)
        a = jnp.where(keep, a * pltpu.roll(a, d, 1), a)
        d *= 2
    h = h_ref[...]
    hs = []
    for g in range(groups):
        hs.append(a[g] * h + u[g])
        h = hs[-1][V7X_SUBLANES - 1:V7X_SUBLANES, :]
    h_ref[...] = h
    hs = jnp.concatenate(hs, axis=0)
    return (hs * _gelu_tanh(gate_ref[...].astype(F32))).astype(BF16)


def _lru_kernel(xin_ref, gate_ref, wax_ref, bax_ref, lam_ref, y_ref, h_ref):
    @pl.when(pl.program_id(1) == 0)
    def _():
        h_ref[...] = jnp.zeros(h_ref.shape, F32)

    y_ref[...] = _lru_block(xin_ref, gate_ref, wax_ref, bax_ref, lam_ref, h_ref)


def _lru(p_lru, p_z, wax, bax, lam, tl):
    bsz, seq, width = p_lru.shape
    row = lambda wd: pl.BlockSpec((1, wd), lambda b, i: (0, 0))
    return pl.pallas_call(
        _lru_kernel,
        grid=(bsz, seq // tl),
        in_specs=[pl.BlockSpec((None, tl, width), lambda b, i: (b, i, 0)),
                  pl.BlockSpec((None, tl, width), lambda b, i: (b, i, 2)),
                  pl.BlockSpec((width, 2 * width), lambda b, i: (0, 0)), row(2 * width), row(width)],
        out_specs=pl.BlockSpec((None, tl, width), lambda b, i: (b, i, 0)),
        out_shape=jax.ShapeDtypeStruct((bsz, seq, width), BF16),
        scratch_shapes=[pltpu.VMEM((1, width), F32)],
        compiler_params=_params("parallel", "arbitrary"),
        name="lru_mixer",
    )(p_lru, p_z, wax, bax, lam)


def _merge_mlp_kernel(ya_ref, yb_ref, lx_ref, lg_ref, lxn_ref, lgn_ref, gl_ref, x_ref, gt1_ref, nw_ref, sc_ref,
                      sh_ref, gt2_ref, wb_ref, wo_ref, wup_ref, wdn_ref, fnw_ref, wax_ref, bax_ref, lam_ref,
                      o_ref, yc_ref, h_ref, *, final, fc):
    d = x_ref.shape[1]
    step = pl.program_id(1)
    slot = step % 2
    lru = functools.partial(_lru_block, wax_ref=wax_ref, bax_ref=bax_ref, lam_ref=lam_ref, h_ref=h_ref)

    @pl.when(step == 0)
    def _():
        h_ref[...] = jnp.zeros(h_ref.shape, F32)
        yc_ref[0] = lru(lx_ref, lg_ref)

    y_c = yc_ref[slot]
    yc_ref[1 - slot] = lru(lxn_ref, lgn_ref)

    merged = None
    for r, y_val in enumerate((ya_ref[...], yb_ref[...], y_c)):
        br = jnp.dot(y_val, wb_ref[r], preferred_element_type=F32)
        term = jax.nn.sigmoid(gl_ref[:, r * d:(r + 1) * d].astype(F32)) * br
        merged = term if merged is None else merged + term
    mix = jnp.dot(merged.astype(BF16), wo_ref[...], preferred_element_type=F32)
    x1 = x_ref[...] + gt1_ref[...] * mix

    h = (_rms(x1) * nw_ref[...] * (1.0 + sc_ref[...]) + sh_ref[...]).astype(BF16)
    acc = None
    for c0 in range(0, wup_ref.shape[1], fc):
        act = jnp.maximum(jnp.dot(h, wup_ref[:, c0:c0 + fc], preferred_element_type=F32), 0.0)
        part = jnp.dot((act * act).astype(BF16), wdn_ref[c0:c0 + fc, :], preferred_element_type=F32)
        acc = part if acc is None else acc + part
    x2 = x1 + gt2_ref[...] * acc
    if final:
        x2 = _rms(x2) * fnw_ref[...]
    o_ref[...] = x2


def _merge_mlp(ya, yb, p_lru, p_z, p_gl, x, gt1, nw, sc, sh, gt2, wb, wo, wup, wdn, fnw, wax, bax, lam, final, tm):
    bsz, seq, d = x.shape
    bw = ya.shape[2]
    last = seq // tm - 1
    tok = lambda width: pl.BlockSpec((None, tm, width), lambda b, i: (b, i, 0))
    nxt = lambda i: jnp.minimum(i + 1, last)
    per_batch = pl.BlockSpec((None, 1, d), lambda b, i: (b, 0, 0))
    row = lambda wd: pl.BlockSpec((1, wd), lambda b, i: (0, 0))
    return pl.pallas_call(
        functools.partial(_merge_mlp_kernel, final=final, fc=1024),
        grid=(bsz, seq // tm),
        in_specs=[tok(bw), tok(bw),
                  tok(bw), pl.BlockSpec((None, tm, bw), lambda b, i: (b, i, 2)),
                  pl.BlockSpec((None, tm, bw), lambda b, i: (b, nxt(i), 0)),
                  pl.BlockSpec((None, tm, bw), lambda b, i: (b, nxt(i), 2)),
                  tok(N_BRANCH * d), tok(d), per_batch, row(d), per_batch, per_batch,
                  per_batch, _resident(wb.shape), _resident(wo.shape), _resident(wup.shape),
                  _resident(wdn.shape), row(d), _resident(wax.shape), row(2 * bw), row(bw)],
        out_specs=tok(d),
        out_shape=jax.ShapeDtypeStruct((bsz, seq, d), F32),
        scratch_shapes=[pltpu.VMEM((2, tm, bw), BF16), pltpu.VMEM((1, bw), F32)],
        compiler_params=_params("parallel", "arbitrary"),
        name="merge_mlp",
    )(ya, yb, p_lru, p_z, p_lru, p_z, p_gl, x, gt1, nw, sc, sh, gt2, wb, wo, wup, wdn, fnw, wax, bax, lam)


def _lane_row(vec, lane0):
    return jnp.pad(vec.astype(F32), (lane0, SMALL_WIDTH - lane0 - vec.shape[0])).reshape(1, SMALL_WIDTH)


def _block_diag(w):
    nb, a, b = w.shape
    eye = jnp.eye(nb, dtype=w.dtype)
    return (eye[:, None, :, None] * w[:, :, None, :]).reshape(nb * a, nb * b)


def kernel(x, c, ada_w, ada_b, norm_mix, w_in, gdn_conv_w, gdn_a_log, gdn_dt_bias, gdn_norm, ssd_conv_w, ssd_conv_b, ssd_a_log, ssd_dt_bias, ssd_d, ssd_norm, lru_conv_w, lru_conv_b, lru_w_a, lru_b_a, lru_w_x, lru_b_x, lru_lambda, w_branch, w_out, norm_mlp, w_up, w_down, final_norm):
    bsz, seq, d = x.shape
    depth = ada_w.shape[0]
    gh = gdn_a_log.shape[1]
    dv = gdn_norm.shape[1]
    gqk = (gdn_conv_w.shape[2] - gh * dv) // 2
    dk = gqk // gh
    sh_ = ssd_a_log.shape[1]
    inner = ssd_norm.shape[1]
    hd = inner // sh_
    sbc = (ssd_conv_w.shape[2] - inner) // 2
    ngroups = SSD_GROUPS
    nstate = sbc // ngroups
    lw = lru_lambda.shape[1]
    assert seq % MIXER_BLOCK == 0 and MIXER_BLOCK % SCAN_CHUNK == 0 and 2 * gh + sh_ <= SMALL_WIDTH
    assert seq % SCAN_BLOCK == 0 and SCAN_BLOCK % SCAN_CHUNK == 0

    splits = (gqk, gqk, gh * dv, gh * dv, gh, gh, inner, inner, sbc, sbc, sh_, lw, lw, N_BRANCH * d)
    offs = np.concatenate([[0], np.cumsum(splits)])
    cols = lambda w, i: w[:, offs[i]:offs[i + 1]]
    ssd_lane0 = 2 * gh
    widths = (2 * gqk + gh * dv, inner + 2 * sbc, lw, gh * dv + inner + lw, N_BRANCH * d, SMALL_WIDTH)
    dtypes = (BF16, BF16, BF16, BF16, BF16, F32)
    pieces = ((widths[0], True, False, True), (widths[1], True, True, True), (widths[2], True, True, False),
              (widths[3], False, False, False), (widths[4], False, False, False), (widths[5], False, False, False))

    mod = _modulation(c, ada_w, ada_b)
    expand = jnp.asarray(np.kron(np.eye(SMALL_WIDTH, sh_, -ssd_lane0), np.ones((1, hd))), BF16)

    tm = 512
    for l in range(depth):
        sh1, sc1, gt1, sh2, sc2, gt2 = [mod[l, :, None, j * d:(j + 1) * d] for j in range(N_MOD)]
        w = w_in[l]
        small = jnp.concatenate([cols(w, 4), cols(w, 5), cols(w, 10)], axis=1)
        w_all = jnp.concatenate(
            [cols(w, 0), cols(w, 1), cols(w, 2),
             cols(w, 6), cols(w, 8), cols(w, 9),
             cols(w, 11),
             cols(w, 3), cols(w, 7), cols(w, 12),
             cols(w, 13),
             jnp.pad(small, ((0, 0), (0, SMALL_WIDTH - small.shape[1])))], axis=1).astype(BF16)
        conv_w = jnp.concatenate([gdn_conv_w[l], ssd_conv_w[l], lru_conv_w[l]], axis=1)
        conv_b = jnp.concatenate([jnp.zeros((widths[0],), F32), ssd_conv_b[l], lru_conv_b[l]]).reshape(1, -1)
        p_gdn, p_ssd, p_lru, p_z, p_gl, p_s = _inproj(
            x, norm_mix[l].reshape(1, d), sc1, sh1, w_all, conv_w, conv_b, pieces, dtypes, tm)

        y_a = _gdn(p_gdn, p_z, p_s, _lane_row(gdn_a_log[l], gh), _lane_row(gdn_dt_bias[l], gh),
                   gdn_norm[l].reshape(1, dv), gh, dk, dv, MIXER_BLOCK)
        y_b = _ssd(p_ssd, p_z, p_s, _lane_row(ssd_a_log[l], ssd_lane0), _lane_row(ssd_dt_bias[l], ssd_lane0),
                   jnp.repeat(ssd_d[l], hd).reshape(1, inner), ssd_norm[l].reshape(1, inner), expand,
                   sh_, hd, ngroups, nstate, ssd_lane0, SCAN_BLOCK)
        wax = jnp.concatenate([_block_diag(lru_w_a[l]), _block_diag(lru_w_x[l])], axis=1).astype(BF16)
        bax = jnp.concatenate([lru_b_a[l], lru_b_x[l]]).reshape(1, 2 * lw)
        x = _merge_mlp(y_a, y_b, p_lru, p_z, p_gl, x, gt1, norm_mlp[l].reshape(1, d), sc2, sh2, gt2,
                       w_branch[l].astype(BF16), w_out[l].astype(BF16), w_up[l].astype(BF16),
                       w_down[l].astype(BF16), final_norm.reshape(1, d), wax, bax,
                       lru_lambda[l].reshape(1, lw), l == depth - 1, tm)
    return x
```
